```python
import jax
import jax.numpy as jnp
from jax import lax
import numpy as np

D_MODEL = 2048
BATCH = 4
SEQ = 2048
DEPTH = 4
DEC_BATCH = 128
DEC_SEQ = 8
PAST_LEN = 16384
PAGE_SIZE = 128

W_GLA = 3 * D_MODEL // 8
W_LRU = 3 * D_MODEL // 8
W_RET = D_MODEL - W_GLA - W_LRU
H_GLA = 4
DV_GLA = W_GLA // H_GLA
DK_GLA = DV_GLA // 2
GLA_RANK = 16
GLA_TAU = 16.0
H_LRU = 8
BLK_LRU = W_LRU // H_LRU
LRU_CONV = 4
LRU_C = 8.0
H_RET = 4
DK_RET = W_RET // H_RET
DV_RET = W_RET // H_RET
ROPE_BASE = 10000.0
D_FF = 5632
FFN_CONV = 3
CHUNK = 64
EPS = 1e-6
IN_SIZES = (H_GLA * DK_GLA, H_GLA * DK_GLA, W_GLA, GLA_RANK, W_GLA, W_LRU, W_LRU, W_RET, W_RET, W_RET, W_RET)
N_IN = sum(IN_SIZES)

kernel_name = "hymba_gla_rglru_retention_convffn_step"


def _rmsnorm(x, g):
    xf = x.astype(jnp.float32)
    y = xf * lax.rsqrt(jnp.mean(xf * xf, axis=-1, keepdims=True) + EPS)
    return y.astype(x.dtype) * g


def _head_rmsnorm(o, g):
    y = o * lax.rsqrt(jnp.mean(o * o, axis=-1, keepdims=True) + EPS)
    B, T, H, Dv = o.shape
    return y.reshape(B, T, H * Dv) * g.astype(jnp.float32)


def _split_cols(z, sizes):
    outs = []
    off = 0
    for s in sizes:
        outs.append(z[..., off:off + s])
        off += s
    return outs


def _causal_dwconv(x, buf, w, b):
    K = w.shape[0]
    T = x.shape[1]
    xe = jnp.concatenate([buf.astype(x.dtype), x], axis=1)
    y = xe[:, 0:T] * w[0] + b
    for k in range(1, K):
        y = y + xe[:, k:k + T] * w[k]
    return y, xe[:, T:]


def _chunk_len(T):
    return CHUNK if T % CHUNK == 0 else T


def _to_chunks(a, C):
    B, T, H, X = a.shape
    return a.reshape(B, T // C, C, H, X).transpose(1, 0, 3, 2, 4)


def _from_chunks(a):
    N, B, H, C, X = a.shape
    return a.transpose(1, 0, 3, 2, 4).reshape(B, N * C, H, X)


def _rope(x, pos):
    half = x.shape[-1] // 2
    freqs = ROPE_BASE ** (-jnp.arange(half, dtype=jnp.float32) / half)
    ang = pos.astype(jnp.float32)[:, None] * freqs[None, :]
    cos = jnp.cos(ang)[None, :, None, :]
    sin = jnp.sin(ang)[None, :, None, :]
    x1 = x[..., :half].astype(jnp.float32)
    x2 = x[..., half:].astype(jnp.float32)
    return jnp.concatenate([x1 * cos - x2 * sin, x1 * sin + x2 * cos], axis=-1).astype(x.dtype)


def _gla(q, k, v, log_a, S0):
    T = q.shape[1]
    C = _chunk_len(T)
    f32 = jnp.float32
    causal = jnp.tril(jnp.ones((C, C), dtype=bool))[:, :, None]

    def step(S, inp):
        qc, kc, vc, ac = inp
        b = jnp.cumsum(ac, axis=2)
        diff = b[:, :, :, None, :] - b[:, :, None, :, :]
        decay = jnp.exp(jnp.where(causal, diff, -jnp.inf))
        att = jnp.einsum('bhtd,bhsd,bhtsd->bhts', qc, kc, decay)
        o = jnp.einsum('bhts,bhsv->bhtv', att, vc) + jnp.einsum('bhtd,bhdv->bhtv', qc * jnp.exp(b), S)
        b_last = b[:, :, -1:, :]
        S = jnp.exp(b_last[:, :, 0, :])[..., None] * S + jnp.einsum('bhsd,bhsv->bhdv', kc * jnp.exp(b_last - b), vc)
        return S, o

    xs = (_to_chunks(q.astype(f32), C), _to_chunks(k.astype(f32), C), _to_chunks(v.astype(f32), C), _to_chunks(log_a.astype(f32), C))
    S, o = lax.scan(step, S0.astype(f32), xs)
    return _from_chunks(o), S.astype(S0.dtype)


def _retention(q, k, v, S0):
    T = q.shape[1]
    H = q.shape[2]
    C = _chunk_len(T)
    f32 = jnp.float32
    log_g = jnp.log1p(-jnp.exp2(-5.0 - jnp.arange(H, dtype=f32)))
    idx = jnp.arange(C, dtype=f32)
    rel = idx[:, None] - idx[None, :]
    decay_in = jnp.where(rel[None] >= 0, jnp.exp(jnp.maximum(rel, 0.0)[None] * log_g[:, None, None]), 0.0)
    q_dec = jnp.exp((idx + 1.0)[None, :] * log_g[:, None])[None, :, :, None]
    k_dec = jnp.exp((C - 1.0 - idx)[None, :] * log_g[:, None])[None, :, :, None]
    c_dec = jnp.exp(C * log_g)[None, :, None, None]

    def step(S, inp):
        qc, kc, vc = inp
        att = jnp.einsum('bhtd,bhsd->bhts', qc, kc) * decay_in[None]
        o = jnp.einsum('bhts,bhsv->bhtv', att, vc) + jnp.einsum('bhtd,bhdv->bhtv', qc, S) * q_dec
        S = c_dec * S + jnp.einsum('bhsd,bhsv->bhdv', kc * k_dec, vc)
        return S, o

    xs = (_to_chunks(q.astype(f32), C), _to_chunks(k.astype(f32), C), _to_chunks(v.astype(f32), C))
    S, o = lax.scan(step, S0.astype(f32), xs)
    return _from_chunks(o), S.astype(S0.dtype)


def _lin_combine(left, right):
    a1, b1 = left
    a2, b2 = right
    return a1 * a2, a2 * b1 + b2


def _rglru(xc, h0, w_a, b_a, w_x, b_x, lam):
    B, T, W = xc.shape
    f32 = jnp.float32
    xf = xc.astype(f32)
    xb = xf.reshape(B, T, H_LRU, W // H_LRU)
    r = jax.nn.sigmoid(jnp.einsum('bthi,hij->bthj', xb, w_a.astype(f32)).reshape(B, T, W) + b_a.astype(f32))
    i = jax.nn.sigmoid(jnp.einsum('bthi,hij->bthj', xb, w_x.astype(f32)).reshape(B, T, W) + b_x.astype(f32))
    log_a = -LRU_C * r * jax.nn.softplus(-lam.astype(f32))
    a = jnp.exp(log_a)
    u = jnp.sqrt(-jnp.expm1(2.0 * log_a)) * (i * xf)
    u = u.at[:, 0].add(a[:, 0] * h0.astype(f32))
    _, h = lax.associative_scan(_lin_combine, (a, u), axis=1)
    return h, h[:, -1].astype(h0.dtype)


def _token_mixers(hn, pos, S_gla, S_ret, h_lru, buf_lru, w_in, gla_w_alpha, gla_b_alpha, gla_norm_g,
                  lru_conv_w, lru_conv_b, lru_w_a, lru_b_a, lru_w_x, lru_b_x, lru_lambda, ret_norm_g, w_out):
    B, T, _ = hn.shape
    f32 = jnp.float32
    z = hn @ w_in
    gq, gk, gv, g_lr, g_gate, l_x, l_gate, rq, rk, rv, r_gate = _split_cols(z, IN_SIZES)
    q = gq.reshape(B, T, H_GLA, DK_GLA) * (DK_GLA ** -0.5)
    k = gk.reshape(B, T, H_GLA, DK_GLA)
    v = gv.reshape(B, T, H_GLA, DV_GLA)
    log_a = jax.nn.log_sigmoid((g_lr @ gla_w_alpha + gla_b_alpha).astype(f32)) / GLA_TAU
    o, S_gla = _gla(q, k, v, log_a.reshape(B, T, H_GLA, DK_GLA), S_gla)
    o_gla = _head_rmsnorm(o, gla_norm_g) * jax.nn.silu(g_gate.astype(f32))
    xc, buf_lru = _causal_dwconv(l_x, buf_lru, lru_conv_w, lru_conv_b)
    h, h_lru = _rglru(xc, h_lru, lru_w_a, lru_b_a, lru_w_x, lru_b_x, lru_lambda)
    o_lru = h * jax.nn.gelu(l_gate.astype(f32))
    q = _rope(rq.reshape(B, T, H_RET, DK_RET), pos) * (DK_RET ** -0.5)
    k = _rope(rk.reshape(B, T, H_RET, DK_RET), pos)
    v = rv.reshape(B, T, H_RET, DV_RET)
    o, S_ret = _retention(q, k, v, S_ret)
    o_ret = _head_rmsnorm(o, ret_norm_g) * jax.nn.silu(r_gate.astype(f32))
    mixed = jnp.concatenate([o_gla, o_lru, o_ret], axis=-1).astype(hn.dtype)
    return mixed @ w_out, S_gla, S_ret, h_lru, buf_lru


def _conv_ffn(hn, buf, w_up, conv_w, conv_b, w_down):
    up = hn @ w_up
    u, buf = _causal_dwconv(up, buf, conv_w, conv_b)
    gate = u[..., :D_FF]
    val = u[..., D_FF:]
    return (jax.nn.silu(gate) * val) @ w_down, buf


def _run_group(x, c, start, st_gla, st_ret, st_lru, st_lru_conv, st_ffn_conv,
               norm1_g, norm2_g, final_g, w_ada, b_ada, w_in, gla_w_alpha, gla_b_alpha, gla_norm_g,
               lru_conv_w, lru_conv_b, lru_w_a, lru_b_a, lru_w_x, lru_b_x, lru_lambda, ret_norm_g, w_out,
               ffn_w_up, ffn_conv_w, ffn_conv_b, ffn_w_down):
    T = x.shape[1]
    pos = start + jnp.arange(T, dtype=jnp.int32)
    n_gla, n_ret, n_lru, n_lconv, n_fconv = [], [], [], [], []
    c_act = jax.nn.silu(c)
    for l in range(DEPTH):
        mod = c_act @ w_ada[l] + b_ada[l]
        sh1, sc1, g1, sh2, sc2, g2 = jnp.split(mod, 6, axis=-1)
        hn = _rmsnorm(x, norm1_g[l]) * (1 + sc1[:, None]) + sh1[:, None]
        m, S_gla, S_ret, h_lru, b_lru = _token_mixers(
            hn, pos, st_gla[l], st_ret[l], st_lru[l], st_lru_conv[l], w_in[l], gla_w_alpha[l], gla_b_alpha[l],
            gla_norm_g[l], lru_conv_w[l], lru_conv_b[l], lru_w_a[l], lru_b_a[l], lru_w_x[l], lru_b_x[l],
            lru_lambda[l], ret_norm_g[l], w_out[l])
        x = x + g1[:, None] * m
        hn = _rmsnorm(x, norm2_g[l]) * (1 + sc2[:, None]) + sh2[:, None]
        f, b_ffn = _conv_ffn(hn, st_ffn_conv[l], ffn_w_up[l], ffn_conv_w[l], ffn_conv_b[l], ffn_w_down[l])
        x = x + g2[:, None] * f
        n_gla.append(S_gla)
        n_ret.append(S_ret)
        n_lru.append(h_lru)
        n_lconv.append(b_lru)
        n_fconv.append(b_ffn)
    y = _rmsnorm(x, final_g)
    return y, jnp.stack(n_gla), jnp.stack(n_ret), jnp.stack(n_lru), jnp.stack(n_lconv), jnp.stack(n_fconv)


def setup_inputs(seed: int = 0) -> dict:
    key = jax.random.key(seed)
    ks = jax.random.split(key, 32)
    f32 = jnp.float32
    D = D_MODEL

    def nrm(k, shape, scale):
        return jax.random.normal(k, shape, f32) * scale

    a_c = jax.random.uniform(ks[24], (DEPTH, W_LRU), f32, 0.9, 0.999)
    a0 = a_c ** (1.0 / LRU_C)
    return {
        'x_prompt': nrm(ks[0], (BATCH, SEQ, D), 1.0),
        'x_sample': nrm(ks[1], (DEC_BATCH, DEC_SEQ, D), 1.0),
        'state_gla': nrm(ks[2], (DEPTH, DEC_BATCH, H_GLA, DK_GLA, DV_GLA), 1.0),
        'state_ret': nrm(ks[3], (DEPTH, DEC_BATCH, H_RET, DK_RET, DV_RET), 1.0),
        'state_lru': nrm(ks[4], (DEPTH, DEC_BATCH, W_LRU), 0.5),
        'state_lru_conv': nrm(ks[5], (DEPTH, DEC_BATCH, LRU_CONV - 1, W_LRU), 1.0),
        'state_ffn_conv': nrm(ks[6], (DEPTH, DEC_BATCH, FFN_CONV - 1, 2 * D_FF), 1.0),
        'c_prompt': nrm(ks[7], (BATCH, D), 1.0),
        'c_sample': nrm(ks[8], (DEC_BATCH, D), 1.0),
        'norm1_g': 1.0 + nrm(ks[9], (DEPTH, D), 0.02),
        'norm2_g': 1.0 + nrm(ks[10], (DEPTH, D), 0.02),
        'final_g': 1.0 + nrm(ks[11], (D,), 0.02),
        'w_ada': nrm(ks[12], (DEPTH, D, 6 * D), 0.5 * D ** -0.5),
        'b_ada': nrm(ks[13], (DEPTH, 6 * D), 0.02),
        'w_in': nrm(ks[14], (DEPTH, D, N_IN), D ** -0.5),
        'gla_w_alpha': nrm(ks[15], (DEPTH, GLA_RANK, H_GLA * DK_GLA), GLA_RANK ** -0.5),
        'gla_b_alpha': nrm(ks[16], (DEPTH, H_GLA * DK_GLA), 0.1),
        'gla_norm_g': 1.0 + nrm(ks[17], (DEPTH, W_GLA), 0.02),
        'lru_conv_w': nrm(ks[18], (DEPTH, LRU_CONV, W_LRU), LRU_CONV ** -0.5),
        'lru_conv_b': nrm(ks[19], (DEPTH, W_LRU), 0.02),
        'lru_w_a': nrm(ks[20], (DEPTH, H_LRU, BLK_LRU, BLK_LRU), BLK_LRU ** -0.5),
        'lru_b_a': nrm(ks[21], (DEPTH, W_LRU), 0.1),
        'lru_w_x': nrm(ks[22], (DEPTH, H_LRU, BLK_LRU, BLK_LRU), BLK_LRU ** -0.5),
        'lru_b_x': nrm(ks[23], (DEPTH, W_LRU), 0.1),
        'lru_lambda': jnp.log(a0) - jnp.log1p(-a0),
        'ret_norm_g': 1.0 + nrm(ks[25], (DEPTH, W_RET), 0.02),
        'w_out': nrm(ks[26], (DEPTH, D, D), D ** -0.5),
        'ffn_w_up': nrm(ks[27], (DEPTH, D, 2 * D_FF), D ** -0.5),
        'ffn_conv_w': nrm(ks[28], (DEPTH, FFN_CONV, 2 * D_FF), FFN_CONV ** -0.5),
        'ffn_conv_b': nrm(ks[29], (DEPTH, 2 * D_FF), 0.02),
        'ffn_w_down': nrm(ks[30], (DEPTH, D_FF, D), D_FF ** -0.5),
    }


def reference(x_prompt, x_sample, state_gla, state_ret, state_lru, state_lru_conv, state_ffn_conv,
              c_prompt, c_sample, norm1_g, norm2_g, final_g, w_ada, b_ada, w_in, gla_w_alpha, gla_b_alpha,
              gla_norm_g, lru_conv_w, lru_conv_b, lru_w_a, lru_b_a, lru_w_x, lru_b_x, lru_lambda, ret_norm_g,
              w_out, ffn_w_up, ffn_conv_w, ffn_conv_b, ffn_w_down):
    B = x_prompt.shape[0]
    dt = x_prompt.dtype
    z_gla = jnp.zeros((DEPTH, B, H_GLA, DK_GLA, DV_GLA), dt)
    z_ret = jnp.zeros((DEPTH, B, H_RET, DK_RET, DV_RET), dt)
    z_lru = jnp.zeros((DEPTH, B, W_LRU), dt)
    z_lconv = jnp.zeros((DEPTH, B, LRU_CONV - 1, W_LRU), dt)
    z_fconv = jnp.zeros((DEPTH, B, FFN_CONV - 1, 2 * D_FF), dt)
    y_prompt, p_gla, p_ret, p_lru, p_lconv, p_fconv = _run_group(
        x_prompt, c_prompt, 0, z_gla, z_ret, z_lru, z_lconv, z_fconv,
        norm1_g, norm2_g, final_g, w_ada, b_ada, w_in, gla_w_alpha, gla_b_alpha, gla_norm_g,
        lru_conv_w, lru_conv_b, lru_w_a, lru_b_a, lru_w_x, lru_b_x, lru_lambda, ret_norm_g, w_out,
        ffn_w_up, ffn_conv_w, ffn_conv_b, ffn_w_down)
    y_sample, s_gla, s_ret, s_lru, s_lconv, s_fconv = _run_group(
        x_sample, c_sample, PAST_LEN, state_gla, state_ret, state_lru, state_lru_conv, state_ffn_conv,
        norm1_g, norm2_g, final_g, w_ada, b_ada, w_in, gla_w_alpha, gla_b_alpha, gla_norm_g,
        lru_conv_w, lru_conv_b, lru_w_a, lru_b_a, lru_w_x, lru_b_x, lru_lambda, ret_norm_g, w_out,
        ffn_w_up, ffn_conv_w, ffn_conv_b, ffn_w_down)
    return (y_prompt, y_sample, p_gla, p_ret, p_lru, p_lconv, p_fconv, s_gla, s_ret, s_lru, s_lconv, s_fconv)
```

```python
import functools
import math

import jax
import jax.numpy as jnp
from jax import lax
from jax.experimental import pallas as pl
from jax.experimental.pallas import tpu as pltpu

F32 = jnp.float32
BF16 = jnp.bfloat16

D = 2048
DEPTH = 4
BP, TP = 4, 2048
BS, TS = 128, 8
PAST = 16384
H_GLA, DK_GLA, DV_GLA = 4, 96, 192
GLA_RANK = 16
GLA_TAU = 16.0
W_LRU, H_LRU, BLK_LRU = 768, 8, 96
LRU_K = 4
LRU_C = 8.0
H_RET, DK_RET = 4, 128
ROPE_BASE = 10000.0
D_FF = 5632
FFN_K = 3
CHUNK = 64
EPS = 1e-6

LANES = 128
SUBLANES = 8
DKP, DVP = 128, 256

Q0 = 0
K0 = Q0 + H_GLA * DKP
V0 = K0 + H_GLA * DKP
LR0 = V0 + H_GLA * DVP
GG0 = LR0 + LANES
LX0 = GG0 + H_GLA * DVP
LG0 = LX0 + W_LRU
RQ0 = LG0 + W_LRU
RK0 = RQ0 + H_RET * DK_RET
RV0 = RK0 + H_RET * DK_RET
RG0 = RV0 + H_RET * DK_RET
NZ_USED = RG0 + H_RET * DK_RET
PROJ_TN = 512
NZ = -(-NZ_USED // PROJ_TN) * PROJ_TN
MG0 = 0
ML0 = H_GLA * DVP
MR0 = ML0 + W_LRU
NMIX = MR0 + H_RET * DK_RET

SUB = 16
NEG = -1e30
FF_TN = 512
FF_NJ = D_FF // FF_TN

VMEM_LIMIT = 56 * 1024 * 1024


def _cparams(sem):
    return pltpu.CompilerParams(dimension_semantics=sem, vmem_limit_bytes=VMEM_LIMIT)


def _sigmoid(x):
    return 1.0 / (1.0 + jnp.exp(-x))


def _silu(x):
    return x * _sigmoid(x)


def _gelu_tanh(x):
    c = math.sqrt(2.0 / math.pi)
    return 0.5 * x * (1.0 + jnp.tanh(c * (x + 0.044715 * (x * x * x))))


def _log_sigmoid(x):
    return jnp.minimum(x, 0.0) - jnp.log1p(jnp.exp(-jnp.abs(x)))


def _softplus(x):
    return jnp.maximum(x, 0.0) + jnp.log1p(jnp.exp(-jnp.abs(x)))


def _dot(a, b):
    return jnp.dot(a, b, preferred_element_type=F32)


def _dot_nt(a, b):
    return lax.dot_general(a, b, (((1,), (1,)), ((), ())), preferred_element_type=F32)


def _dot_tn(a, b):
    return lax.dot_general(a, b, (((0,), (0,)), ((), ())), preferred_element_type=F32)


def _norm_mod(x, g, sc, sh):
    ms = jnp.mean(x * x, axis=-1, keepdims=True)
    return (x * lax.rsqrt(ms + EPS)) * g * (1.0 + sc) + sh


def _head_norm(o, width):
    ms = jnp.sum(o * o, axis=-1, keepdims=True) * (1.0 / width)
    return o * lax.rsqrt(ms + EPS)


def _mod_kernel(cp_ref, cs_ref, w_ref, b_ref, op_ref, os_ref):
    w = w_ref[...].astype(BF16)
    b = b_ref[...]
    op_ref[...] = _dot(_silu(cp_ref[...]).astype(BF16), w) + b
    os_ref[...] = _dot(_silu(cs_ref[...]).astype(BF16), w) + b


def _mod_call(cp8, cs, w_ada, b_ada):
    tn = 1024
    n = 6 * D
    return pl.pallas_call(
        _mod_kernel,
        grid=(DEPTH, n // tn),
        in_specs=[
            pl.BlockSpec((SUBLANES, D), lambda l, j: (0, 0)),
            pl.BlockSpec((BS, D), lambda l, j: (0, 0)),
            pl.BlockSpec((None, D, tn), lambda l, j: (l, 0, j)),
            pl.BlockSpec((None, 1, tn), lambda l, j: (l, 0, j)),
        ],
        out_specs=[
            pl.BlockSpec((None, SUBLANES, tn), lambda l, j: (l, 0, j)),
            pl.BlockSpec((None, BS, tn), lambda l, j: (l, 0, j)),
        ],
        out_shape=[
            jax.ShapeDtypeStruct((DEPTH, SUBLANES, n), F32),
            jax.ShapeDtypeStruct((DEPTH, BS, n), F32),
        ],
        compiler_params=_cparams(("arbitrary", "arbitrary")),
        name="adaln_mod",
    )(cp8, cs, w_ada, b_ada)


ROW_CHUNK = 128


def _fill_hn(x_ref, g_ref, sc_ref, sh_ref, hn_ref, *, tm, prompt, seq_tiles):
    g = g_ref[...]
    if prompt:
        b = pl.program_id(0) // seq_tiles
        sc = sc_ref[pl.ds(b, 1), :]
        sh = sh_ref[pl.ds(b, 1), :]

    def body(r, carry):
        rows = pl.ds(pl.multiple_of(r * ROW_CHUNK, ROW_CHUNK), ROW_CHUNK)
        if prompt:
            y = _norm_mod(x_ref[rows, :], g, sc, sh)
        else:
            y = _norm_mod(x_ref[rows, :], g, sc_ref[...], sh_ref[...])
        hn_ref[rows, :] = y.astype(BF16)
        return carry

    lax.fori_loop(0, tm // ROW_CHUNK, body, 0)


def _mod_specs(layer, chunks, prompt, width, col_of):
    rows = SUBLANES if prompt else BS
    per = D // width
    return [pl.BlockSpec((None, rows, width), (lambda i, j, c=c: (layer, 0, c * per + col_of(i, j)))) for c in chunks]


def _proj_kernel(x_ref, g_ref, sc_ref, sh_ref, w_ref, o_ref, hn_ref, *, tm, prompt, seq_tiles):
    @pl.when(pl.program_id(1) == 0)
    def _():
        _fill_hn(x_ref, g_ref, sc_ref, sh_ref, hn_ref, tm=tm, prompt=prompt, seq_tiles=seq_tiles)

    o_ref[...] = _dot(hn_ref[...], w_ref[...])


def _proj_call(x, norm_g, mod, w, *, layer, sc_chunk, sh_chunk, prompt, tm, tn):
    m = x.shape[0]
    n = w.shape[-1]
    kern = functools.partial(_proj_kernel, tm=tm, prompt=prompt, seq_tiles=TP // tm if prompt else 1)
    sc_spec, sh_spec = _mod_specs(layer, (sc_chunk, sh_chunk), prompt, D, lambda i, j: 0)
    return pl.pallas_call(
        kern,
        grid=(m // tm, n // tn),
        in_specs=[
            pl.BlockSpec((tm, D), lambda i, j: (i, 0)),
            pl.BlockSpec((None, 1, D), lambda i, j: (layer, 0, 0)),
            sc_spec,
            sh_spec,
            pl.BlockSpec((None, D, tn), lambda i, j: (layer, 0, j)),
        ],
        out_specs=pl.BlockSpec((tm, tn), lambda i, j: (i, j)),
        out_shape=jax.ShapeDtypeStruct((m, n), F32),
        scratch_shapes=[pltpu.VMEM((tm, D), BF16)],
        compiler_params=_cparams(("arbitrary", "arbitrary")),
        name="in_proj_p" if prompt else "in_proj_s",
    )(x, norm_g, mod, mod, w)


def _resid_kernel(a_ref, w_ref, x_ref, g_ref, o_ref, *, tm, prompt, seq_tiles):
    acc = _dot(a_ref[...].astype(BF16), w_ref[...])
    if prompt:
        b = pl.program_id(0) // seq_tiles
        o_ref[...] = x_ref[...] + g_ref[pl.ds(b, 1), :] * acc
    else:
        g = g_ref[...]
        for t in range(tm // BS):
            rows = slice(t * BS, (t + 1) * BS)
            o_ref[rows, :] = x_ref[rows, :] + g * acc[rows, :]


def _resid_call(a, w, x, mod, *, layer, g_chunk, prompt, tm, tn, name):
    m, k = a.shape
    kern = functools.partial(_resid_kernel, tm=tm, prompt=prompt, seq_tiles=TP // tm if prompt else 1)
    (g_spec,) = _mod_specs(layer, (g_chunk,), prompt, tn, lambda i, j: j)
    return pl.pallas_call(
        kern,
        grid=(m // tm, D // tn),
        in_specs=[
            pl.BlockSpec((tm, k), lambda i, j: (i, 0)),
            pl.BlockSpec((None, k, tn), lambda i, j: (layer, 0, j)),
            pl.BlockSpec((tm, tn), lambda i, j: (i, j)),
            g_spec,
        ],
        out_specs=pl.BlockSpec((tm, tn), lambda i, j: (i, j)),
        out_shape=jax.ShapeDtypeStruct((m, D), F32),
        compiler_params=_cparams(("arbitrary", "arbitrary")),
        name=name,
    )(a, w, x, mod)


def _ffn_up_kernel(*refs, tm, prompt, seq_tiles, pre, shift):
    if prompt:
        x_ref, g_ref, sc_ref, sh_ref, w_ref, cw_ref, cb_ref, act_ref, st_ref, hn_ref, ue_ref, carry_ref = refs
    else:
        (x_ref, g_ref, sc_ref, sh_ref, w_ref, cw_ref, cb_ref, s0_ref, s1_ref,
         act_ref, st_ref, hn_ref, ue_ref) = refs
    i = pl.program_id(0)
    j = pl.program_id(1)

    @pl.when(j == 0)
    def _():
        _fill_hn(x_ref, g_ref, sc_ref, sh_ref, hn_ref, tm=tm, prompt=prompt, seq_tiles=seq_tiles)

    ue_ref[pre:pre + tm, :] = _dot(hn_ref[...], w_ref[...])
    if prompt:
        first = (i % seq_tiles) == 0

        @pl.when(first)
        def _():
            ue_ref[0:pre, :] = jnp.zeros((pre, 2 * FF_TN), F32)

        @pl.when(jnp.logical_not(first))
        def _():
            ue_ref[0:pre, :] = carry_ref[j]

        carry_ref[j] = ue_ref[tm:tm + pre, :]
        st_ref[...] = ue_ref[pre + tm - (FFN_K - 1):pre + tm, :]
    else:
        ue_ref[0:BS, :] = s0_ref[...]
        ue_ref[BS:2 * BS, :] = s1_ref[...]
        st_ref[0] = ue_ref[pre + tm - 2 * BS:pre + tm - BS, :]
        st_ref[1] = ue_ref[pre + tm - BS:pre + tm, :]

    cw = cw_ref[...]
    cb = cb_ref[...]
    rc = 128
    for r in range(tm // rc):
        base = r * rc
        u = cb + cw[0:1, :] * ue_ref[base + pre - 2 * shift:base + pre - 2 * shift + rc, :]
        u = u + cw[1:2, :] * ue_ref[base + pre - shift:base + pre - shift + rc, :]
        u = u + cw[2:3, :] * ue_ref[base + pre:base + pre + rc, :]
        act_ref[base:base + rc, :] = (_silu(u[:, :FF_TN]) * u[:, FF_TN:]).astype(BF16)


def _ffn_up_call(x, norm_g, mod, w, cw, cb, st_in, *, layer, prompt, tm):
    m = x.shape[0]
    seq_tiles = TP // tm if prompt else 1
    pre = SUBLANES if prompt else 2 * BS
    shift = 1 if prompt else BS
    kern = functools.partial(_ffn_up_kernel, tm=tm, prompt=prompt, seq_tiles=seq_tiles, pre=pre, shift=shift)
    sc_spec, sh_spec = _mod_specs(layer, (4, 3), prompt, D, lambda i, j: 0)
    in_specs = [
        pl.BlockSpec((tm, D), lambda i, j: (i, 0)),
        pl.BlockSpec((None, 1, D), lambda i, j: (layer, 0, 0)),
        sc_spec,
        sh_spec,
        pl.BlockSpec((None, D, 2 * FF_TN), lambda i, j: (layer, 0, j)),
        pl.BlockSpec((None, FFN_K, 2 * FF_TN), lambda i, j: (layer, 0, j)),
        pl.BlockSpec((None, 1, 2 * FF_TN), lambda i, j: (layer, 0, j)),
    ]
    args = [x, norm_g, mod, mod, w, cw, cb]
    scratch = [pltpu.VMEM((tm, D), BF16), pltpu.VMEM((pre + tm, 2 * FF_TN), F32)]
    if prompt:
        st_spec = pl.BlockSpec((None, FFN_K - 1, 2 * FF_TN), lambda i, j: (i, 0, j))
        st_shape = jax.ShapeDtypeStruct((m // tm, FFN_K - 1, 2 * D_FF), F32)
        scratch.append(pltpu.VMEM((FF_NJ, pre, 2 * FF_TN), F32))
    else:
        in_specs += [pl.BlockSpec((None, None, BS, 2 * FF_TN), lambda i, j, k=k: (layer, k, 0, j)) for k in range(2)]
        args += [st_in, st_in]
        st_spec = pl.BlockSpec((FFN_K - 1, BS, 2 * FF_TN), lambda i, j: (0, 0, j))
        st_shape = jax.ShapeDtypeStruct((FFN_K - 1, BS, 2 * D_FF), F32)
    return pl.pallas_call(
        kern,
        grid=(m // tm, FF_NJ),
        in_specs=in_specs,
        out_specs=[pl.BlockSpec((tm, FF_TN), lambda i, j: (i, j)), st_spec],
        out_shape=[jax.ShapeDtypeStruct((m, D_FF), BF16), st_shape],
        scratch_shapes=scratch,
        compiler_params=_cparams(("arbitrary", "arbitrary")),
        name="ffn_up_p" if prompt else "ffn_up_s",
    )(*args)


def _final_kernel(x_ref, g_ref, o_ref):
    x = x_ref[...]
    ms = jnp.mean(x * x, axis=-1, keepdims=True)
    o_ref[...] = (x * lax.rsqrt(ms + EPS)) * g_ref[...]


def _final_call(x, g):
    m = x.shape[0]
    tm = 256
    return pl.pallas_call(
        _final_kernel,
        grid=(m // tm,),
        in_specs=[pl.BlockSpec((tm, D), lambda i: (i, 0)), pl.BlockSpec((1, D), lambda i: (0, 0))],
        out_specs=pl.BlockSpec((tm, D), lambda i: (i, 0)),
        out_shape=jax.ShapeDtypeStruct((m, D), F32),
        compiler_params=_cparams(("arbitrary",)),
        name="final_norm",
    )(x, g)


def _split3(x):
    x1 = x.astype(BF16)
    r = x - x1.astype(F32)
    x2 = r.astype(BF16)
    r = r - x2.astype(F32)
    return x1, x2, r.astype(BF16)


def _ret_log_gamma(h):
    return math.log1p(-(2.0 ** (-5.0 - h)))


def _lru_gates(xc, wa_ref, ba_ref, wx_ref, bx_ref, lam_ref):
    xb = xc.astype(BF16)
    r = _sigmoid(_dot(xb, wa_ref[...]) + ba_ref[...])
    i = _sigmoid(_dot(xb, wx_ref[...]) + bx_ref[...])
    log_a = (-LRU_C * r) * _softplus(-lam_ref[...])
    a = jnp.exp(log_a)
    u = jnp.sqrt(-jnp.tanh(log_a) * (a * a + 1.0)) * (i * xc)
    return a, u


MIX_TC = 256


def _mixer_prompt_kernel(z_ref, cos_ref, sin_ref, wal_ref, bal_ref, gng_ref, lcw_ref, lcb_ref,
                         wa_ref, ba_ref, wx_ref, bx_ref, lam_ref, rng_ref,
                         mix_ref, sg_ref, sr_ref, hl_ref, lc_ref,
                         stg_ref, str_ref, h_ref, lxe_ref, a_ref, u_ref, hs_ref,
                         q_s, k_s, b_s, att_s):
    t_id = pl.program_id(1)
    nt = pl.num_programs(1)
    tc = MIX_TC
    pre = SUBLANES

    @pl.when(t_id == 0)
    def _():
        stg_ref[...] = jnp.zeros_like(stg_ref)
        str_ref[...] = jnp.zeros_like(str_ref)
        h_ref[...] = jnp.zeros_like(h_ref)
        lxe_ref[0:pre, :] = jnp.zeros((pre, W_LRU), F32)

    lxe_ref[pre:pre + tc, :] = z_ref[:, LX0:LX0 + W_LRU]

    row_c = lax.broadcasted_iota(jnp.int32, (CHUNK, CHUNK), 0)
    col_c = lax.broadcasted_iota(jnp.int32, (CHUNK, CHUNK), 1)
    tri = (col_c <= row_c).astype(BF16)
    rel = (row_c - col_c).astype(F32)
    rowf = lax.broadcasted_iota(jnp.int32, (CHUNK, LANES), 0).astype(F32)
    row1 = lax.broadcasted_iota(jnp.int32, (CHUNK, 1), 0)
    sub_row = lax.broadcasted_iota(jnp.int32, (SUB, 1), 0)
    lane_id = lax.broadcasted_iota(jnp.int32, (SUB, LANES), 1)

    def chunk_body(c, carry):
        r0 = pl.multiple_of(c * CHUNK, CHUNK)
        rows = pl.ds(r0, CHUNK)

        zq = z_ref[rows, Q0:Q0 + H_GLA * DKP] * (DK_GLA ** -0.5)
        zk = z_ref[rows, K0:K0 + H_GLA * DKP]
        lr = z_ref[rows, LR0:LR0 + LANES].astype(BF16)
        la = _log_sigmoid(_dot(lr, wal_ref[...]) + bal_ref[...]) * (1.0 / GLA_TAU)
        p1, p2, p3 = _split3(la)
        bcs = _dot(tri, p1) + _dot(tri, p2) + _dot(tri, p3)
        q_s[...] = zq
        k_s[...] = zk
        b_s[...] = bcs
        b_last = bcs[CHUNK - 1:CHUNK, :]
        qb = (zq * jnp.exp(bcs)).astype(BF16)
        kb = (zk * jnp.exp(b_last - bcs)).astype(BF16)
        e_last = jnp.exp(b_last)

        for i in range(CHUNK // SUB):
            blk = slice(i * SUB, (i + 1) * SUB)
            q_blk = zq[blk, :]
            b_blk = bcs[blk, :]
            if i > 0:
                r_i = bcs[i * SUB - 1:i * SUB, :]
                q_i = (q_blk * jnp.exp(b_blk - r_i)).astype(BF16)
                kk = (zk * jnp.exp(jnp.where(row1 < i * SUB, r_i - bcs, NEG))).astype(BF16)

            def s_body(s, att):
                ks = k_s[pl.ds(s, 1), :]
                bs = b_s[pl.ds(s, 1), :]
                w = q_blk * ks * jnp.exp(jnp.where(sub_row + i * SUB >= s, b_blk - bs, NEG))
                out = []
                for h in range(H_GLA):
                    col = jnp.sum(w[:, h * DKP:(h + 1) * DKP], axis=-1, keepdims=True)
                    out.append(jnp.where(lane_id == s, col, att[h]))
                return tuple(out)

            att0 = tuple(jnp.zeros((SUB, LANES), F32) for _ in range(H_GLA))
            att = lax.fori_loop(i * SUB, (i + 1) * SUB, s_body, att0)
            for h in range(H_GLA):
                a_h = att[h][:, :CHUNK]
                if i > 0:
                    a_h = a_h + _dot_nt(q_i[:, h * DKP:(h + 1) * DKP], kk[:, h * DKP:(h + 1) * DKP])
                att_s[h, blk, 0:CHUNK] = a_h

        for h in range(H_GLA):
            v_h = z_ref[rows, V0 + h * DVP:V0 + (h + 1) * DVP].astype(BF16)
            st_old = stg_ref[h]
            o_h = _dot(att_s[h, :, 0:CHUNK].astype(BF16), v_h)
            o_h = o_h + _dot_nt(qb[:, h * DKP:(h + 1) * DKP], st_old.astype(BF16))
            stg_ref[h] = e_last[:, h * DKP:(h + 1) * DKP] * st_old + _dot_tn(v_h, kb[:, h * DKP:(h + 1) * DKP])
            gate = z_ref[rows, GG0 + h * DVP:GG0 + (h + 1) * DVP]
            y = _head_norm(o_h, DV_GLA) * gng_ref[:, h * DVP:(h + 1) * DVP] * _silu(gate)
            mix_ref[rows, MG0 + h * DVP:MG0 + (h + 1) * DVP] = y.astype(BF16)

        cosv = cos_ref[rows, :]
        sinv = sin_ref[rows, :]
        for h in range(H_RET):
            lg = _ret_log_gamma(h)
            cols = slice(h * DK_RET, (h + 1) * DK_RET)
            xq = z_ref[rows, RQ0 + h * DK_RET:RQ0 + (h + 1) * DK_RET]
            xk = z_ref[rows, RK0 + h * DK_RET:RK0 + (h + 1) * DK_RET]
            q = (xq * cosv + pltpu.roll(xq, DK_RET // 2, axis=1) * sinv) * (DK_RET ** -0.5)
            k = xk * cosv + pltpu.roll(xk, DK_RET // 2, axis=1) * sinv
            v = z_ref[rows, RV0 + h * DK_RET:RV0 + (h + 1) * DK_RET].astype(BF16)
            decay = jnp.where(rel >= 0, jnp.exp(jnp.maximum(rel, 0.0) * lg), 0.0)
            q_dec = jnp.exp((rowf + 1.0) * lg)
            k_dec = jnp.exp((CHUNK - 1.0 - rowf) * lg)
            c_dec = math.exp(CHUNK * lg)
            qb16 = q.astype(BF16)
            att_r = _dot_nt(qb16, k.astype(BF16)) * decay
            s_old = str_ref[h]
            o = _dot(att_r.astype(BF16), v) + _dot(qb16, s_old.astype(BF16)) * q_dec
            str_ref[h] = c_dec * s_old + _dot_tn((k * k_dec).astype(BF16), v)
            gate = z_ref[rows, RG0 + h * DK_RET:RG0 + (h + 1) * DK_RET]
            y = _head_norm(o, DK_RET) * rng_ref[:, cols] * _silu(gate)
            mix_ref[rows, MR0 + h * DK_RET:MR0 + (h + 1) * DK_RET] = y.astype(BF16)

        return carry

    lax.fori_loop(0, tc // CHUNK, chunk_body, 0)

    cw = lcw_ref[...]
    xc = lcb_ref[...] + cw[0:1, :] * lxe_ref[pre - 3:pre - 3 + tc, :]
    xc = xc + cw[1:2, :] * lxe_ref[pre - 2:pre - 2 + tc, :]
    xc = xc + cw[2:3, :] * lxe_ref[pre - 1:pre - 1 + tc, :]
    xc = xc + cw[3:4, :] * lxe_ref[pre:pre + tc, :]
    a, u = _lru_gates(xc, wa_ref, ba_ref, wx_ref, bx_ref, lam_ref)
    a_ref[...] = a
    u_ref[...] = u

    def scan_body(t, h):
        h = a_ref[pl.ds(t, 1), :] * h + u_ref[pl.ds(t, 1), :]
        hs_ref[pl.ds(t, 1), :] = h
        return h

    h_fin = lax.fori_loop(0, tc, scan_body, h_ref[0:1, :], unroll=8)
    h_ref[0:1, :] = h_fin

    def out_body(c, carry):
        rows = pl.ds(pl.multiple_of(c * CHUNK, CHUNK), CHUNK)
        y = hs_ref[rows, :] * _gelu_tanh(z_ref[rows, LG0:LG0 + W_LRU])
        mix_ref[rows, ML0:ML0 + W_LRU] = y.astype(BF16)
        return carry

    lax.fori_loop(0, tc // CHUNK, out_body, 0)

    lxe_ref[0:pre, :] = lxe_ref[tc:tc + pre, :]

    @pl.when(t_id == nt - 1)
    def _():
        for h in range(H_GLA):
            sg_ref[h] = stg_ref[h].T[:DK_GLA, :DV_GLA]
        sr_ref[...] = str_ref[...]
        hl_ref[...] = h_fin
        lc_ref[...] = lxe_ref[tc + pre - (LRU_K - 1):tc + pre, :]


def _mixer_prompt_call(z, cos_t, sin_t, pw, *, layer):
    tc = MIX_TC
    nt = TP // tc

    def lspec(shape):
        return pl.BlockSpec((None,) + shape, lambda b, t: (layer,) + (0,) * len(shape))

    in_specs = [
        pl.BlockSpec((tc, NZ), lambda b, t: (b * nt + t, 0)),
        pl.BlockSpec((tc, LANES), lambda b, t: (t, 0)),
        pl.BlockSpec((tc, LANES), lambda b, t: (t, 0)),
        lspec((LANES, H_GLA * DKP)), lspec((1, H_GLA * DKP)), lspec((1, H_GLA * DVP)),
        lspec((LRU_K, W_LRU)), lspec((1, W_LRU)),
        lspec((W_LRU, W_LRU)), lspec((1, W_LRU)), lspec((W_LRU, W_LRU)), lspec((1, W_LRU)), lspec((1, W_LRU)),
        lspec((1, H_RET * DK_RET)),
    ]
    out_specs = [
        pl.BlockSpec((tc, NMIX), lambda b, t: (b * nt + t, 0)),
        pl.BlockSpec((None, H_GLA, DK_GLA, DV_GLA), lambda b, t: (b, 0, 0, 0)),
        pl.BlockSpec((None, H_RET, DK_RET, DK_RET), lambda b, t: (b, 0, 0, 0)),
        pl.BlockSpec((None, 1, W_LRU), lambda b, t: (b, 0, 0)),
        pl.BlockSpec((None, LRU_K - 1, W_LRU), lambda b, t: (b, 0, 0)),
    ]
    out_shape = [
        jax.ShapeDtypeStruct((BP * TP, NMIX), BF16),
        jax.ShapeDtypeStruct((BP, H_GLA, DK_GLA, DV_GLA), F32),
        jax.ShapeDtypeStruct((BP, H_RET, DK_RET, DK_RET), F32),
        jax.ShapeDtypeStruct((BP, 1, W_LRU), F32),
        jax.ShapeDtypeStruct((BP, LRU_K - 1, W_LRU), F32),
    ]
    scratch = [
        pltpu.VMEM((H_GLA, DVP, DKP), F32),
        pltpu.VMEM((H_RET, DK_RET, DK_RET), F32),
        pltpu.VMEM((SUBLANES, W_LRU), F32),
        pltpu.VMEM((SUBLANES + tc, W_LRU), F32),
        pltpu.VMEM((tc, W_LRU), F32), pltpu.VMEM((tc, W_LRU), F32), pltpu.VMEM((tc, W_LRU), F32),
        pltpu.VMEM((CHUNK, H_GLA * DKP), F32), pltpu.VMEM((CHUNK, H_GLA * DKP), F32),
        pltpu.VMEM((CHUNK, H_GLA * DKP), F32),
        pltpu.VMEM((H_GLA, CHUNK, LANES), F32),
    ]
    return pl.pallas_call(
        _mixer_prompt_kernel,
        grid=(BP, nt),
        in_specs=in_specs,
        out_specs=out_specs,
        out_shape=out_shape,
        scratch_shapes=scratch,
        compiler_params=_cparams(("arbitrary", "arbitrary")),
        name="mixer_p",
    )(z, cos_t, sin_t, pw["w_alpha"], pw["b_alpha"], pw["gla_norm_g"], pw["lru_conv_w"], pw["lru_conv_b"],
      pw["lru_wa"], pw["lru_ba"], pw["lru_wx"], pw["lru_bx"], pw["lru_lam"], pw["ret_norm_g"])


MIX_BB = 8


def _mixer_sample_kernel(z_ref, cos_ref, sin_ref, wal_ref, bal_ref, gng_ref, lcw_ref, lcb_ref,
                         wa_ref, ba_ref, wx_ref, bx_ref, lam_ref, rng_ref,
                         sg0_ref, sr0_ref, h0_ref, lc0_ref,
                         mix_ref, sg_ref, sr_ref, hl_ref, lc_ref,
                         qb_s, kb_s, v_s, og_s, rq_s, rk_s, rv_s, or_s, sp_ref, et_ref):
    bb = MIX_BB
    nrow = TS * bb

    @pl.when(pl.program_id(0) == 0)
    def _():
        sp_ref[...] = jnp.zeros_like(sp_ref)
        et_ref[...] = jnp.zeros_like(et_ref)

    def zcols(c0, width):
        return z_ref[:, :, c0:c0 + width].reshape(nrow, width)

    def slab(x, t):
        return x[t * bb:(t + 1) * bb, :]

    zq = zcols(Q0, H_GLA * DKP) * (DK_GLA ** -0.5)
    zk = zcols(K0, H_GLA * DKP)
    zv = zcols(V0, H_GLA * DVP)
    la = _log_sigmoid(_dot(zcols(LR0, LANES).astype(BF16), wal_ref[...]) + bal_ref[...]) * (1.0 / GLA_TAU)
    bs = [slab(la, 0)]
    for t in range(1, TS):
        bs.append(bs[-1] + slab(la, t))
    b_last = bs[-1]
    e_last = jnp.exp(b_last)
    o_t = []
    for t in range(TS):
        acc = None
        for s in range(t + 1):
            w = slab(zq, t) * slab(zk, s) * jnp.exp(bs[t] - bs[s])
            parts = []
            for h in range(H_GLA):
                a_ts = jnp.sum(w[:, h * DKP:(h + 1) * DKP], axis=-1, keepdims=True)
                parts.append(a_ts * slab(zv, s)[:, h * DVP:(h + 1) * DVP])
            contrib = jnp.concatenate(parts, axis=1)
            acc = contrib if acc is None else acc + contrib
        o_t.append(acc)
    o_intra = jnp.concatenate(o_t, axis=0)
    b_all = jnp.concatenate(bs, axis=0)
    qb = zq * jnp.exp(b_all)
    kb = zk * jnp.exp(jnp.concatenate([b_last] * TS, axis=0) - b_all)
    for h in range(H_GLA):
        qb_s[h] = qb[:, h * DKP:(h + 1) * DKP]
        kb_s[h] = kb[:, h * DKP:(h + 1) * DKP]
        for p in range(DVP // LANES):
            v_s[h * (DVP // LANES) + p] = zv[:, h * DVP + p * LANES:h * DVP + (p + 1) * LANES]
        et_ref[h, 0:bb, :] = e_last[:, h * DKP:(h + 1) * DKP]
    e_cols = [et_ref[h].T for h in range(H_GLA)]

    for b in range(bb):
        seq = pl.ds(b, TS, stride=bb)
        for h in range(H_GLA):
            s0 = sg0_ref[b, h]
            sp_ref[0:DK_GLA, 0:DV_GLA] = s0
            q_bh = qb_s[h, seq, :]
            k_bh = kb_s[h, seq, :]
            v_bh = jnp.concatenate([v_s[h * (DVP // LANES) + p, seq, :] for p in range(DVP // LANES)], axis=1)
            o_bh = _dot(q_bh, sp_ref[...])
            for p in range(DVP // LANES):
                og_s[h * (DVP // LANES) + p, seq, :] = o_bh[:, p * LANES:(p + 1) * LANES]
            upd = _dot_tn(k_bh, v_bh)
            s_new = e_cols[h][:, b:b + 1] * sp_ref[...] + upd
            sg_ref[b, h] = s_new[:DK_GLA, :DV_GLA]

    og = jnp.concatenate([og_s[i] for i in range(H_GLA * DVP // LANES)], axis=1) + o_intra
    for h in range(H_GLA):
        cols = slice(h * DVP, (h + 1) * DVP)
        y = _head_norm(og[:, cols], DV_GLA) * gng_ref[:, cols] * _silu(zcols(GG0 + h * DVP, DVP))
        mix_ref[:, :, MG0 + h * DVP:MG0 + (h + 1) * DVP] = y.reshape(TS, bb, DVP)

    cosv = jnp.concatenate([jnp.broadcast_to(cos_ref[t:t + 1, :], (bb, LANES)) for t in range(TS)], axis=0)
    sinv = jnp.concatenate([jnp.broadcast_to(sin_ref[t:t + 1, :], (bb, LANES)) for t in range(TS)], axis=0)
    trow = lax.broadcasted_iota(jnp.int32, (TS, LANES), 0).astype(F32)
    r_intra = []
    for h in range(H_RET):
        lg = _ret_log_gamma(h)
        xq = zcols(RQ0 + h * DK_RET, DK_RET)
        xk = zcols(RK0 + h * DK_RET, DK_RET)
        q = (xq * cosv + pltpu.roll(xq, DK_RET // 2, axis=1) * sinv) * (DK_RET ** -0.5)
        k = xk * cosv + pltpu.roll(xk, DK_RET // 2, axis=1) * sinv
        v = zcols(RV0 + h * DK_RET, DK_RET)
        rq_s[h] = q
        rk_s[h] = k
        rv_s[h] = v
        outs = []
        for t in range(TS):
            acc = None
            for s in range(t + 1):
                a_ts = jnp.sum(slab(q, t) * slab(k, s), axis=-1, keepdims=True) * math.exp((t - s) * lg)
                contrib = a_ts * slab(v, s)
                acc = contrib if acc is None else acc + contrib
            outs.append(acc)
        r_intra.append(jnp.concatenate(outs, axis=0))

    for b in range(bb):
        seq = pl.ds(b, TS, stride=bb)
        for h in range(H_RET):
            lg = _ret_log_gamma(h)
            q_dec = jnp.exp((trow + 1.0) * lg)
            k_dec = jnp.exp((TS - 1.0 - trow) * lg)
            s0 = sr0_ref[b, h]
            q_bh = rq_s[h, seq, :]
            k_bh = rk_s[h, seq, :]
            v_bh = rv_s[h, seq, :]
            or_s[h, seq, :] = _dot(q_bh, s0) * q_dec
            sr_ref[b, h] = math.exp(TS * lg) * s0 + _dot_tn(k_bh * k_dec, v_bh)

    for h in range(H_RET):
        cols = slice(h * DK_RET, (h + 1) * DK_RET)
        o = or_s[h] + r_intra[h]
        y = _head_norm(o, DK_RET) * rng_ref[:, cols] * _silu(zcols(RG0 + h * DK_RET, DK_RET))
        mix_ref[:, :, MR0 + h * DK_RET:MR0 + (h + 1) * DK_RET] = y.reshape(TS, bb, DK_RET)

    lx = zcols(LX0, W_LRU)
    xe = [lc0_ref[:, k * W_LRU:(k + 1) * W_LRU] for k in range(LRU_K - 1)] + [slab(lx, t) for t in range(TS)]
    cw = lcw_ref[...]
    xc = []
    for t in range(TS):
        acc = lcb_ref[...] + cw[0:1, :] * xe[t]
        for k in range(1, LRU_K):
            acc = acc + cw[k:k + 1, :] * xe[t + k]
        xc.append(acc)
    a, u = _lru_gates(jnp.concatenate(xc, axis=0), wa_ref, ba_ref, wx_ref, bx_ref, lam_ref)
    h = h0_ref[...]
    hs = []
    for t in range(TS):
        h = slab(a, t) * h + slab(u, t)
        hs.append(h)
    y = jnp.concatenate(hs, axis=0) * _gelu_tanh(zcols(LG0, W_LRU))
    mix_ref[:, :, ML0:ML0 + W_LRU] = y.reshape(TS, bb, W_LRU)
    hl_ref[...] = h
    for k in range(LRU_K - 1):
        lc_ref[:, k * W_LRU:(k + 1) * W_LRU] = xe[TS + k]


def _mixer_sample_call(z, cos_t, sin_t, pw, st_gla, st_ret, st_lru, st_lconv, *, layer):
    bb = MIX_BB

    def lspec(shape):
        return pl.BlockSpec((None,) + shape, lambda j: (layer,) + (0,) * len(shape))

    in_specs = [
        pl.BlockSpec((TS, bb, NZ), lambda j: (0, j, 0)),
        pl.BlockSpec((TS, LANES), lambda j: (0, 0)),
        pl.BlockSpec((TS, LANES), lambda j: (0, 0)),
        lspec((LANES, H_GLA * DKP)), lspec((1, H_GLA * DKP)), lspec((1, H_GLA * DVP)),
        lspec((LRU_K, W_LRU)), lspec((1, W_LRU)),
        lspec((W_LRU, W_LRU)), lspec((1, W_LRU)), lspec((W_LRU, W_LRU)), lspec((1, W_LRU)), lspec((1, W_LRU)),
        lspec((1, H_RET * DK_RET)),
        pl.BlockSpec((None, bb, H_GLA, DK_GLA, DV_GLA), lambda j: (layer, j, 0, 0, 0)),
        pl.BlockSpec((None, bb, H_RET, DK_RET, DK_RET), lambda j: (layer, j, 0, 0, 0)),
        pl.BlockSpec((None, bb, W_LRU), lambda j: (layer, j, 0)),
        pl.BlockSpec((None, bb, (LRU_K - 1) * W_LRU), lambda j: (layer, j, 0)),
    ]
    out_specs = [
        pl.BlockSpec((TS, bb, NMIX), lambda j: (0, j, 0)),
        pl.BlockSpec((bb, H_GLA, DK_GLA, DV_GLA), lambda j: (j, 0, 0, 0)),
        pl.BlockSpec((bb, H_RET, DK_RET, DK_RET), lambda j: (j, 0, 0, 0)),
        pl.BlockSpec((bb, W_LRU), lambda j: (j, 0)),
        pl.BlockSpec((bb, (LRU_K - 1) * W_LRU), lambda j: (j, 0)),
    ]
    out_shape = [
        jax.ShapeDtypeStruct((TS, BS, NMIX), F32),
        jax.ShapeDtypeStruct((BS, H_GLA, DK_GLA, DV_GLA), F32),
        jax.ShapeDtypeStruct((BS, H_RET, DK_RET, DK_RET), F32),
        jax.ShapeDtypeStruct((BS, W_LRU), F32),
        jax.ShapeDtypeStruct((BS, (LRU_K - 1) * W_LRU), F32),
    ]
    nrow = TS * bb
    scratch = [
        pltpu.VMEM((H_GLA, nrow, LANES), F32), pltpu.VMEM((H_GLA, nrow, LANES), F32),
        pltpu.VMEM((H_GLA * DVP // LANES, nrow, LANES), F32), pltpu.VMEM((H_GLA * DVP // LANES, nrow, LANES), F32),
        pltpu.VMEM((H_RET, nrow, LANES), F32), pltpu.VMEM((H_RET, nrow, LANES), F32),
        pltpu.VMEM((H_RET, nrow, LANES), F32), pltpu.VMEM((H_RET, nrow, LANES), F32),
        pltpu.VMEM((DKP, DVP), F32),
        pltpu.VMEM((H_GLA, LANES, LANES), F32),
    ]
    return pl.pallas_call(
        _mixer_sample_kernel,
        grid=(BS // bb,),
        in_specs=in_specs,
        out_specs=out_specs,
        out_shape=out_shape,
        scratch_shapes=scratch,
        compiler_params=_cparams(("arbitrary",)),
        name="mixer_s",
    )(z, cos_t, sin_t, pw["w_alpha"], pw["b_alpha"], pw["gla_norm_g"], pw["lru_conv_w"], pw["lru_conv_b"],
      pw["lru_wa"], pw["lru_ba"], pw["lru_wx"], pw["lru_bx"], pw["lru_lam"], pw["ret_norm_g"],
      st_gla, st_ret, st_lru, st_lconv)


def _pad_heads(w, heads, d, dp):
    lead = w.shape[:-1]
    w = w.reshape(lead + (heads, d))
    w = jnp.pad(w, [(0, 0)] * len(lead) + [(0, 0), (0, dp - d)])
    return w.reshape(lead + (heads * dp,))


def _split_cols(w, sizes):
    out, off = [], 0
    for s in sizes:
        out.append(w[..., off:off + s])
        off += s
    return out


def _ff_perm(w):
    lead = w.shape[:-1]
    w = w.reshape(lead + (2, FF_NJ, FF_TN))
    w = jnp.swapaxes(w, -3, -2)
    return w.reshape(lead + (2 * D_FF,))


def _ff_unperm(w):
    lead = w.shape[:-1]
    w = w.reshape(lead + (FF_NJ, 2, FF_TN))
    w = jnp.swapaxes(w, -3, -2)
    return w.reshape(lead + (2 * D_FF,))


def _rope_tables(start, length):
    half = DK_RET // 2
    freqs = ROPE_BASE ** (-jnp.arange(half, dtype=F32) / half)
    pos = start + jnp.arange(length, dtype=jnp.int32)
    ang = pos.astype(F32)[:, None] * freqs[None, :]
    cos, sin = jnp.cos(ang), jnp.sin(ang)
    return jnp.concatenate([cos, cos], axis=1), jnp.concatenate([-sin, sin], axis=1)


def _prep_weights(w_in, gla_w_alpha, gla_b_alpha, gla_norm_g, lru_conv_w, lru_conv_b, lru_w_a, lru_b_a,
                  lru_w_x, lru_b_x, lru_lambda, ret_norm_g, w_out, ffn_w_up, ffn_conv_w, ffn_conv_b, ffn_w_down):
    w_gla, w_ret = H_GLA * DV_GLA, H_RET * DK_RET
    sizes = (H_GLA * DK_GLA, H_GLA * DK_GLA, w_gla, GLA_RANK, w_gla, W_LRU, W_LRU, w_ret, w_ret, w_ret, w_ret)
    gq, gk, gv, glr, gg, lx, lg, rq, rk, rv, rg = _split_cols(w_in, sizes)
    w_in_p = jnp.concatenate([
        _pad_heads(gq, H_GLA, DK_GLA, DKP), _pad_heads(gk, H_GLA, DK_GLA, DKP), _pad_heads(gv, H_GLA, DV_GLA, DVP),
        jnp.pad(glr, ((0, 0), (0, 0), (0, LANES - GLA_RANK))), _pad_heads(gg, H_GLA, DV_GLA, DVP),
        lx, lg, rq, rk, rv, rg, jnp.zeros((DEPTH, D, NZ - NZ_USED), F32)], axis=-1).astype(BF16)
    w_alpha = jnp.pad(_pad_heads(gla_w_alpha, H_GLA, DK_GLA, DKP), ((0, 0), (0, LANES - GLA_RANK), (0, 0))).astype(BF16)
    eye = jnp.eye(H_LRU, dtype=F32)

    def block_diag(w):
        return jnp.einsum("lhij,hg->lhigj", w, eye).reshape(DEPTH, W_LRU, W_LRU).astype(BF16)

    wo_g, wo_l, wo_r = _split_cols(jnp.swapaxes(w_out, 1, 2), (w_gla, W_LRU, w_ret))
    w_out_p = jnp.swapaxes(jnp.concatenate([_pad_heads(wo_g, H_GLA, DV_GLA, DVP), wo_l, wo_r], axis=-1), 1, 2)
    return dict(
        w_in=w_in_p,
        w_alpha=w_alpha,
        b_alpha=_pad_heads(gla_b_alpha, H_GLA, DK_GLA, DKP)[:, None, :],
        gla_norm_g=_pad_heads(gla_norm_g, H_GLA, DV_GLA, DVP)[:, None, :],
        lru_conv_w=lru_conv_w,
        lru_conv_b=lru_conv_b[:, None, :],
        lru_wa=block_diag(lru_w_a), lru_ba=lru_b_a[:, None, :],
        lru_wx=block_diag(lru_w_x), lru_bx=lru_b_x[:, None, :],
        lru_lam=lru_lambda[:, None, :],
        ret_norm_g=ret_norm_g[:, None, :],
        w_out=w_out_p.astype(BF16),
        w_up=_ff_perm(ffn_w_up).astype(BF16),
        ffn_cw=_ff_perm(ffn_conv_w),
        ffn_cb=_ff_perm(ffn_conv_b)[:, None, :],
        w_down=ffn_w_down.astype(BF16),
    )


def _run_group(x, mod, pw, norm1_g, norm2_g, tables, states, *, prompt):
    tm = 1024 if prompt else TS * BS
    tm_ff = 512 if prompt else TS * BS
    outs = []
    for l in range(DEPTH):
        z = _proj_call(x, norm1_g, mod, pw["w_in"], layer=l, sc_chunk=1, sh_chunk=0, prompt=prompt, tm=tm, tn=PROJ_TN)
        if prompt:
            mixed, s_gla, s_ret, s_lru, s_lconv = _mixer_prompt_call(z, *tables, pw, layer=l)
            s_lru = s_lru.reshape(BP, W_LRU)
        else:
            st_gla, st_ret, st_lru, st_lconv, _ = states
            mixed, s_gla, s_ret, s_lru, s_lconv = _mixer_sample_call(
                z.reshape(TS, BS, NZ), *tables, pw, st_gla, st_ret, st_lru, st_lconv, layer=l)
            mixed = mixed.reshape(TS * BS, NMIX)
            s_lconv = s_lconv.reshape(BS, LRU_K - 1, W_LRU)
        x = _resid_call(mixed, pw["w_out"], x, mod, layer=l, g_chunk=2, prompt=prompt, tm=tm, tn=512,
                        name="out_proj_p" if prompt else "out_proj_s")
        act, s_fconv = _ffn_up_call(x, norm2_g, mod, pw["w_up"], pw["ffn_cw"], pw["ffn_cb"],
                                    None if prompt else states[4], layer=l, prompt=prompt, tm=tm_ff)
        x = _resid_call(act, pw["w_down"], x, mod, layer=l, g_chunk=5, prompt=prompt, tm=tm_ff, tn=512,
                        name="ffn_down_p" if prompt else "ffn_down_s")
        s_fconv = _ff_unperm(s_fconv)
        if prompt:
            s_fconv = s_fconv[TP // tm_ff - 1::TP // tm_ff]
        else:
            s_fconv = jnp.swapaxes(s_fconv, 0, 1)
        outs.append((s_gla, s_ret, s_lru, s_lconv, s_fconv))
    return x, [jnp.stack(o) for o in zip(*outs)]


def kernel(x_prompt, x_sample, state_gla, state_ret, state_lru, state_lru_conv, state_ffn_conv, c_prompt, c_sample,
           norm1_g, norm2_g, final_g, w_ada, b_ada, w_in, gla_w_alpha, gla_b_alpha, gla_norm_g, lru_conv_w,
           lru_conv_b, lru_w_a, lru_b_a, lru_w_x, lru_b_x, lru_lambda, ret_norm_g, w_out, ffn_w_up, ffn_conv_w,
           ffn_conv_b, ffn_w_down):
    pw = _prep_weights(w_in, gla_w_alpha, gla_b_alpha, gla_norm_g, lru_conv_w, lru_conv_b, lru_w_a, lru_b_a,
                       lru_w_x, lru_b_x, lru_lambda, ret_norm_g, w_out, ffn_w_up, ffn_conv_w, ffn_conv_b, ffn_w_down)
    cp8 = jnp.pad(c_prompt, ((0, SUBLANES - BP), (0, 0)))
    mod_p, mod_s = _mod_call(cp8, c_sample, w_ada, b_ada[:, None, :])
    n1 = norm1_g[:, None, :]
    n2 = norm2_g[:, None, :]

    xp = x_prompt.reshape(BP * TP, D)
    xs = jnp.swapaxes(x_sample, 0, 1).reshape(TS * BS, D)
    st_s = (state_gla, state_ret, state_lru, state_lru_conv.reshape(DEPTH, BS, (LRU_K - 1) * W_LRU),
            _ff_perm(jnp.swapaxes(state_ffn_conv, 1, 2)))

    xp, outs_p = _run_group(xp, mod_p, pw, n1, n2, _rope_tables(0, TP), None, prompt=True)
    xs, outs_s = _run_group(xs, mod_s, pw, n1, n2, _rope_tables(PAST, TS), st_s, prompt=False)

    fg = final_g[None, :]
    y_p = _final_call(xp, fg).reshape(BP, TP, D)
    y_s = jnp.swapaxes(_final_call(xs, fg).reshape(TS, BS, D), 0, 1)
    return (y_p, y_s, *outs_p, *outs_s)
```

```python
import functools
import math

import jax
import jax.numpy as jnp
from jax import lax
from jax.experimental import pallas as pl
from jax.experimental.pallas import tpu as pltpu

F32 = jnp.float32
BF16 = jnp.bfloat16

D = 2048
DEPTH = 4
BP, TP = 4, 2048
BS, TS = 128, 8
PAST = 16384
H_GLA, DK_GLA, DV_GLA = 4, 96, 192
GLA_RANK = 16
GLA_TAU = 16.0
W_LRU, H_LRU, BLK_LRU = 768, 8, 96
LRU_K = 4
LRU_C = 8.0
H_RET, DK_RET = 4, 128
ROPE_BASE = 10000.0
D_FF = 5632
FFN_K = 3
CHUNK = 64
EPS = 1e-6

LANES = 128
SUBLANES = 8
DKP, DVP = 128, 256

Q0 = 0
K0 = Q0 + H_GLA * DKP
V0 = K0 + H_GLA * DKP
LR0 = V0 + H_GLA * DVP
GG0 = LR0 + LANES
LX0 = GG0 + H_GLA * DVP
LG0 = LX0 + W_LRU
RQ0 = LG0 + W_LRU
RK0 = RQ0 + H_RET * DK_RET
RV0 = RK0 + H_RET * DK_RET
RG0 = RV0 + H_RET * DK_RET
NZ_USED = RG0 + H_RET * DK_RET
NZ = -(-NZ_USED // 1024) * 1024
MG0 = 0
ML0 = H_GLA * DVP
MR0 = ML0 + W_LRU
NMIX = MR0 + H_RET * DK_RET

SUB = 16
NEG = -1e30
FF_TN = 512
FF_NJ = D_FF // FF_TN

VMEM_LIMIT = 56 * 1024 * 1024


def _cparams(sem):
    return pltpu.CompilerParams(dimension_semantics=sem, vmem_limit_bytes=VMEM_LIMIT)


def _sigmoid(x):
    return 1.0 / (1.0 + jnp.exp(-x))


def _silu(x):
    return x * _sigmoid(x)


def _gelu_tanh(x):
    c = math.sqrt(2.0 / math.pi)
    return 0.5 * x * (1.0 + jnp.tanh(c * (x + 0.044715 * (x * x * x))))


def _log_sigmoid(x):
    return jnp.minimum(x, 0.0) - jnp.log1p(jnp.exp(-jnp.abs(x)))


def _softplus(x):
    return jnp.maximum(x, 0.0) + jnp.log1p(jnp.exp(-jnp.abs(x)))


def _dot(a, b):
    return jnp.dot(a, b, preferred_element_type=F32)


def _dot_nt(a, b):
    return lax.dot_general(a, b, (((1,), (1,)), ((), ())), preferred_element_type=F32)


def _dot_tn(a, b):
    return lax.dot_general(a, b, (((0,), (0,)), ((), ())), preferred_element_type=F32)


def _norm_mod(x, g, sc, sh):
    ms = jnp.mean(x * x, axis=-1, keepdims=True)
    return (x * lax.rsqrt(ms + EPS)) * g * (1.0 + sc) + sh


def _head_norm(o, width):
    ms = jnp.sum(o * o, axis=-1, keepdims=True) * (1.0 / width)
    return o * lax.rsqrt(ms + EPS)


def _mod_kernel(cp_ref, cs_ref, w_ref, b_ref, op_ref, os_ref):
    w = w_ref[...].astype(BF16)
    b = b_ref[...]
    op_ref[...] = _dot(_silu(cp_ref[...]).astype(BF16), w) + b
    os_ref[...] = _dot(_silu(cs_ref[...]).astype(BF16), w) + b


def _mod_call(cp8, cs, w_ada, b_ada):
    tn = 1024
    n = 6 * D
    return pl.pallas_call(
        _mod_kernel,
        grid=(DEPTH, n // tn),
        in_specs=[
            pl.BlockSpec((SUBLANES, D), lambda l, j: (0, 0)),
            pl.BlockSpec((BS, D), lambda l, j: (0, 0)),
            pl.BlockSpec((None, D, tn), lambda l, j: (l, 0, j)),
            pl.BlockSpec((None, 1, tn), lambda l, j: (l, 0, j)),
        ],
        out_specs=[
            pl.BlockSpec((None, SUBLANES, tn), lambda l, j: (l, 0, j)),
            pl.BlockSpec((None, BS, tn), lambda l, j: (l, 0, j)),
        ],
        out_shape=[
            jax.ShapeDtypeStruct((DEPTH, SUBLANES, n), F32),
            jax.ShapeDtypeStruct((DEPTH, BS, n), F32),
        ],
        compiler_params=_cparams(("arbitrary", "arbitrary")),
        name="adaln_mod",
    )(cp8, cs, w_ada, b_ada)


ROW_CHUNK = 128


def _fill_hn(x_ref, g_ref, sc_ref, sh_ref, hn_ref, *, tile, tm, prompt, seq_tiles):
    g = g_ref[...]
    if prompt:
        b = tile // seq_tiles
        sc = sc_ref[pl.ds(b, 1), :]
        sh = sh_ref[pl.ds(b, 1), :]

    def body(r, carry):
        rows = pl.ds(pl.multiple_of(r * ROW_CHUNK, ROW_CHUNK), ROW_CHUNK)
        if prompt:
            y = _norm_mod(x_ref[rows, :], g, sc, sh)
        else:
            y = _norm_mod(x_ref[rows, :], g, sc_ref[...], sh_ref[...])
        hn_ref[rows, :] = y.astype(BF16)
        return carry

    lax.fori_loop(0, tm // ROW_CHUNK, body, 0)


def _mod_specs(layer, chunks, prompt, width, col_of):
    rows = SUBLANES if prompt else BS
    per = D // width
    return [pl.BlockSpec((None, rows, width), (lambda i, j, c=c: (layer, 0, c * per + col_of(i, j)))) for c in chunks]


def _proj_kernel(x_ref, g_ref, sc_ref, sh_ref, w_ref, o_ref, hn_ref, *, tm, prompt, seq_tiles):
    @pl.when(pl.program_id(1) == 0)
    def _():
        _fill_hn(x_ref, g_ref, sc_ref, sh_ref, hn_ref, tile=pl.program_id(0), tm=tm, prompt=prompt,
                 seq_tiles=seq_tiles)

    o_ref[...] = _dot(hn_ref[...], w_ref[...])


def _proj_call(x, norm_g, mod, w, *, layer, sc_chunk, sh_chunk, prompt, tm, tn):
    m = x.shape[0]
    n = w.shape[-1]
    kern = functools.partial(_proj_kernel, tm=tm, prompt=prompt, seq_tiles=TP // tm if prompt else 1)
    sc_spec, sh_spec = _mod_specs(layer, (sc_chunk, sh_chunk), prompt, D, lambda i, j: 0)
    return pl.pallas_call(
        kern,
        grid=(m // tm, n // tn),
        in_specs=[
            pl.BlockSpec((tm, D), lambda i, j: (i, 0)),
            pl.BlockSpec((None, 1, D), lambda i, j: (layer, 0, 0)),
            sc_spec,
            sh_spec,
            pl.BlockSpec((None, D, tn), lambda i, j: (layer, 0, j)),
        ],
        out_specs=pl.BlockSpec((tm, tn), lambda i, j: (i, j)),
        out_shape=jax.ShapeDtypeStruct((m, n), F32),
        scratch_shapes=[pltpu.VMEM((tm, D), BF16)],
        compiler_params=_cparams(("arbitrary", "arbitrary")),
        name="in_proj_p" if prompt else "in_proj_s",
    )(x, norm_g, mod, mod, w)


def _resid_kernel(a_ref, w_ref, x_ref, g_ref, o_ref, *, tm, prompt, seq_tiles):
    acc = _dot(a_ref[...].astype(BF16), w_ref[...])
    if prompt:
        b = pl.program_id(0) // seq_tiles
        o_ref[...] = x_ref[...] + g_ref[pl.ds(b, 1), :] * acc
    else:
        g = g_ref[...]
        for t in range(tm // BS):
            rows = slice(t * BS, (t + 1) * BS)
            o_ref[rows, :] = x_ref[rows, :] + g * acc[rows, :]


def _resid_call(a, w, x, mod, *, layer, g_chunk, prompt, tm, tn, name):
    m, k = a.shape
    kern = functools.partial(_resid_kernel, tm=tm, prompt=prompt, seq_tiles=TP // tm if prompt else 1)
    (g_spec,) = _mod_specs(layer, (g_chunk,), prompt, tn, lambda i, j: j)
    return pl.pallas_call(
        kern,
        grid=(m // tm, D // tn),
        in_specs=[
            pl.BlockSpec((tm, k), lambda i, j: (i, 0)),
            pl.BlockSpec((None, k, tn), lambda i, j: (layer, 0, j)),
            pl.BlockSpec((tm, tn), lambda i, j: (i, j)),
            g_spec,
        ],
        out_specs=pl.BlockSpec((tm, tn), lambda i, j: (i, j)),
        out_shape=jax.ShapeDtypeStruct((m, D), F32),
        compiler_params=_cparams(("arbitrary", "arbitrary")),
        name=name,
    )(a, w, x, mod)


def _ffn_up_kernel(*refs, tm, prompt, seq_tiles, pre, shift, n_steps):
    (x_ref, g_ref, sc_ref, sh_ref, wg_ref, wv_ref, cwg_ref, cwv_ref, cbg_ref, cbv_ref) = refs[:10]
    if prompt:
        act_ref, stg_ref, stv_ref, hn_ref, ue_a, ue_b, carry_ref = refs[10:]
    else:
        s0g_ref, s1g_ref, s0v_ref, s1v_ref, act_ref, stg_ref, stv_ref, hn_ref, ue_a, ue_b = refs[10:]
    s = pl.program_id(0)
    s_mm = jnp.minimum(s, n_steps - 1)
    s_ep = jnp.maximum(s - 1, 0)
    gate_cols = slice(0, FF_TN)
    val_cols = slice(FF_TN, 2 * FF_TN)

    @pl.when(s == 0)
    def _():
        ue_b[...] = jnp.zeros_like(ue_b)

    @pl.when(jnp.logical_and(s_mm % FF_NJ == 0, s < n_steps))
    def _():
        _fill_hn(x_ref, g_ref, sc_ref, sh_ref, hn_ref, tile=s_mm // FF_NJ, tm=tm, prompt=prompt,
                 seq_tiles=seq_tiles)

    def step(ue_mm, ue_ep):
        if prompt:
            first = ((s_ep // FF_NJ) % seq_tiles) == 0

            @pl.when(first)
            def _():
                ue_ep[0:pre, :] = jnp.zeros((pre, 2 * FF_TN), F32)

            @pl.when(jnp.logical_not(first))
            def _():
                ue_ep[0:pre, :] = carry_ref[s_ep % FF_NJ]

        hn = hn_ref[...]
        ue_mm[pre:pre + tm, gate_cols] = _dot(hn, wg_ref[...])
        ue_mm[pre:pre + tm, val_cols] = _dot(hn, wv_ref[...])
        if not prompt:
            ue_ep[0:BS, gate_cols] = s0g_ref[...]
            ue_ep[0:BS, val_cols] = s0v_ref[...]
            ue_ep[BS:2 * BS, gate_cols] = s1g_ref[...]
            ue_ep[BS:2 * BS, val_cols] = s1v_ref[...]

        def conv(base, cols, cw, cb):
            u = cb + cw[0:1, :] * ue_ep[base + pre - 2 * shift:base + pre - 2 * shift + rc, cols]
            u = u + cw[1:2, :] * ue_ep[base + pre - shift:base + pre - shift + rc, cols]
            return u + cw[2:3, :] * ue_ep[base + pre:base + pre + rc, cols]

        rc = 16
        cwg, cwv, cbg, cbv = cwg_ref[...], cwv_ref[...], cbg_ref[...], cbv_ref[...]
        for r in range(tm // rc):
            gate = conv(r * rc, gate_cols, cwg, cbg)
            val = conv(r * rc, val_cols, cwv, cbv)
            act_ref[r * rc:(r + 1) * rc, :] = (_silu(gate) * val).astype(BF16)
        if prompt:
            carry_ref[s_ep % FF_NJ] = ue_ep[tm:tm + pre, :]
            stg_ref[...] = ue_ep[pre + tm - (FFN_K - 1):pre + tm, gate_cols]
            stv_ref[...] = ue_ep[pre + tm - (FFN_K - 1):pre + tm, val_cols]
        else:
            for k in range(FFN_K - 1):
                rows = slice(pre + tm - (FFN_K - 1 - k) * BS, pre + tm - (FFN_K - 2 - k) * BS)
                stg_ref[k] = ue_ep[rows, gate_cols]
                stv_ref[k] = ue_ep[rows, val_cols]

    @pl.when(s % 2 == 0)
    def _():
        step(ue_a, ue_b)

    @pl.when(s % 2 == 1)
    def _():
        step(ue_b, ue_a)


def _ffn_up_call(x, norm_g, mod, w, cw, cb, st_in, *, layer, prompt, tm):
    m = x.shape[0]
    seq_tiles = TP // tm if prompt else 1
    pre = SUBLANES if prompt else (FFN_K - 1) * BS
    shift = 1 if prompt else BS
    n_steps = (m // tm) * FF_NJ
    kern = functools.partial(_ffn_up_kernel, tm=tm, prompt=prompt, seq_tiles=seq_tiles, pre=pre, shift=shift,
                             n_steps=n_steps)

    def mm_blk(s):
        return jnp.minimum(s, n_steps - 1)

    def ep_blk(s):
        return jnp.maximum(s - 1, 0)

    mod_rows = SUBLANES if prompt else BS
    in_specs = [
        pl.BlockSpec((tm, D), lambda s: (mm_blk(s) // FF_NJ, 0)),
        pl.BlockSpec((None, 1, D), lambda s: (layer, 0, 0)),
        pl.BlockSpec((None, mod_rows, D), lambda s: (layer, 0, 4)),
        pl.BlockSpec((None, mod_rows, D), lambda s: (layer, 0, 3)),
        pl.BlockSpec((None, D, FF_TN), lambda s: (layer, 0, mm_blk(s) % FF_NJ)),
        pl.BlockSpec((None, D, FF_TN), lambda s: (layer, 0, FF_NJ + mm_blk(s) % FF_NJ)),
        pl.BlockSpec((None, FFN_K, FF_TN), lambda s: (layer, 0, ep_blk(s) % FF_NJ)),
        pl.BlockSpec((None, FFN_K, FF_TN), lambda s: (layer, 0, FF_NJ + ep_blk(s) % FF_NJ)),
        pl.BlockSpec((None, 1, FF_TN), lambda s: (layer, 0, ep_blk(s) % FF_NJ)),
        pl.BlockSpec((None, 1, FF_TN), lambda s: (layer, 0, FF_NJ + ep_blk(s) % FF_NJ)),
    ]
    args = [x, norm_g, mod, mod, w, w, cw, cw, cb, cb]
    scratch = [pltpu.VMEM((tm, D), BF16), pltpu.VMEM((pre + tm, 2 * FF_TN), F32),
               pltpu.VMEM((pre + tm, 2 * FF_TN), F32)]
    if prompt:
        st_spec = pl.BlockSpec((None, FFN_K - 1, FF_TN), lambda s: (ep_blk(s) // FF_NJ, 0, ep_blk(s) % FF_NJ))
        st_shape = jax.ShapeDtypeStruct((m // tm, FFN_K - 1, D_FF), F32)
        scratch.append(pltpu.VMEM((FF_NJ, pre, 2 * FF_TN), F32))
    else:
        for half in range(2):
            in_specs += [pl.BlockSpec((None, None, BS, FF_TN),
                                      lambda s, k=k, half=half: (layer, k, 0, half * FF_NJ + ep_blk(s) % FF_NJ))
                         for k in range(FFN_K - 1)]
            args += [st_in] * (FFN_K - 1)
        st_spec = pl.BlockSpec((FFN_K - 1, BS, FF_TN), lambda s: (0, 0, ep_blk(s) % FF_NJ))
        st_shape = jax.ShapeDtypeStruct((FFN_K - 1, BS, D_FF), F32)
    return pl.pallas_call(
        kern,
        grid=(n_steps + 1,),
        in_specs=in_specs,
        out_specs=[pl.BlockSpec((tm, FF_TN), lambda s: (ep_blk(s) // FF_NJ, ep_blk(s) % FF_NJ)), st_spec, st_spec],
        out_shape=[jax.ShapeDtypeStruct((m, D_FF), BF16), st_shape, st_shape],
        scratch_shapes=scratch,
        compiler_params=_cparams(("arbitrary",)),
        name="ffn_up_p" if prompt else "ffn_up_s",
    )(*args)


def _final_kernel(x_ref, g_ref, o_ref):
    x = x_ref[...]
    ms = jnp.mean(x * x, axis=-1, keepdims=True)
    o_ref[...] = (x * lax.rsqrt(ms + EPS)) * g_ref[...]


def _final_call(x, g):
    m = x.shape[0]
    tm = 256
    return pl.pallas_call(
        _final_kernel,
        grid=(m // tm,),
        in_specs=[pl.BlockSpec((tm, D), lambda i: (i, 0)), pl.BlockSpec((1, D), lambda i: (0, 0))],
        out_specs=pl.BlockSpec((tm, D), lambda i: (i, 0)),
        out_shape=jax.ShapeDtypeStruct((m, D), F32),
        compiler_params=_cparams(("arbitrary",)),
        name="final_norm",
    )(x, g)


def _split3(x):
    x1 = x.astype(BF16)
    r = x - x1.astype(F32)
    x2 = r.astype(BF16)
    r = r - x2.astype(F32)
    return x1, x2, r.astype(BF16)


def _ret_log_gamma(h):
    return math.log1p(-(2.0 ** (-5.0 - h)))


def _lru_gates(xc, wa_ref, ba_ref, wx_ref, bx_ref, lam_ref):
    xb = xc.astype(BF16)
    r = _sigmoid(_dot(xb, wa_ref[...]) + ba_ref[...])
    i = _sigmoid(_dot(xb, wx_ref[...]) + bx_ref[...])
    log_a = (-LRU_C * r) * _softplus(-lam_ref[...])
    a = jnp.exp(log_a)
    u = jnp.sqrt(-jnp.tanh(log_a) * (a * a + 1.0)) * (i * xc)
    return a, u


MIX_TC = 256


def _mixer_prompt_kernel(z_ref, cos_ref, sin_ref, wal_ref, bal_ref, gng_ref, lcw_ref, lcb_ref,
                         wa_ref, ba_ref, wx_ref, bx_ref, lam_ref, rng_ref,
                         mix_ref, sg_ref, sr_ref, hl_ref, lc_ref,
                         stg_ref, str_ref, h_ref, lxe_ref, a_ref, u_ref, hs_ref, att_s):
    t_id = pl.program_id(1)
    nt = pl.num_programs(1)
    tc = MIX_TC
    pre = SUBLANES

    @pl.when(t_id == 0)
    def _():
        stg_ref[...] = jnp.zeros_like(stg_ref)
        str_ref[...] = jnp.zeros_like(str_ref)
        h_ref[...] = jnp.zeros_like(h_ref)
        lxe_ref[0:pre, :] = jnp.zeros((pre, W_LRU), F32)

    lxe_ref[pre:pre + tc, :] = z_ref[:, LX0:LX0 + W_LRU]

    row_c = lax.broadcasted_iota(jnp.int32, (CHUNK, CHUNK), 0)
    col_c = lax.broadcasted_iota(jnp.int32, (CHUNK, CHUNK), 1)
    tri = (col_c <= row_c).astype(BF16)
    rel = (row_c - col_c).astype(F32)
    rowf = lax.broadcasted_iota(jnp.int32, (CHUNK, LANES), 0).astype(F32)
    row1 = lax.broadcasted_iota(jnp.int32, (CHUNK, 1), 0)
    sub_row = lax.broadcasted_iota(jnp.int32, (SUB, 1), 0)
    lane_id = lax.broadcasted_iota(jnp.int32, (SUB, LANES), 1)

    def chunk_body(c, carry):
        r0 = pl.multiple_of(c * CHUNK, CHUNK)
        rows = pl.ds(r0, CHUNK)

        zq = z_ref[rows, Q0:Q0 + H_GLA * DKP] * (DK_GLA ** -0.5)
        zk = z_ref[rows, K0:K0 + H_GLA * DKP]
        lr = z_ref[rows, LR0:LR0 + LANES].astype(BF16)
        la = _log_sigmoid(_dot(lr, wal_ref[...]) + bal_ref[...]) * (1.0 / GLA_TAU)
        p1, p2, p3 = _split3(la)
        bcs = _dot(tri, p1) + _dot(tri, p2) + _dot(tri, p3)
        b_last = bcs[CHUNK - 1:CHUNK, :]
        qb = (zq * jnp.exp(bcs)).astype(BF16)
        kb = (zk * jnp.exp(b_last - bcs)).astype(BF16)
        e_last = jnp.exp(b_last)

        for i in range(CHUNK // SUB):
            blk = slice(i * SUB, (i + 1) * SUB)
            q_blk = zq[blk, :]
            b_blk = bcs[blk, :]
            if i > 0:
                r_i = bcs[i * SUB - 1:i * SUB, :]
                q_i = (q_blk * jnp.exp(b_blk - r_i)).astype(BF16)
                kk = (zk * jnp.exp(jnp.where(row1 < i * SUB, r_i - bcs, NEG))).astype(BF16)

            att = [jnp.zeros((SUB, LANES), F32) for _ in range(H_GLA)]
            for sl in range(SUB):
                s = i * SUB + sl
                diff = b_blk - bcs[s:s + 1, :]
                if sl > 0:
                    diff = jnp.where(sub_row >= sl, diff, NEG)
                w = q_blk * zk[s:s + 1, :] * jnp.exp(diff)
                for h in range(H_GLA):
                    col = jnp.sum(w[:, h * DKP:(h + 1) * DKP], axis=-1, keepdims=True)
                    att[h] = jnp.where(lane_id == s, col, att[h])
            for h in range(H_GLA):
                a_h = att[h][:, :CHUNK]
                if i > 0:
                    a_h = a_h + _dot_nt(q_i[:, h * DKP:(h + 1) * DKP], kk[:, h * DKP:(h + 1) * DKP])
                att_s[h, blk, 0:CHUNK] = a_h

        for h in range(H_GLA):
            v_h = z_ref[rows, V0 + h * DVP:V0 + (h + 1) * DVP].astype(BF16)
            st_old = stg_ref[h]
            o_h = _dot(att_s[h, :, 0:CHUNK].astype(BF16), v_h)
            o_h = o_h + _dot_nt(qb[:, h * DKP:(h + 1) * DKP], st_old.astype(BF16))
            stg_ref[h] = e_last[:, h * DKP:(h + 1) * DKP] * st_old + _dot_tn(v_h, kb[:, h * DKP:(h + 1) * DKP])
            gate = z_ref[rows, GG0 + h * DVP:GG0 + (h + 1) * DVP]
            y = _head_norm(o_h, DV_GLA) * gng_ref[:, h * DVP:(h + 1) * DVP] * _silu(gate)
            mix_ref[rows, MG0 + h * DVP:MG0 + (h + 1) * DVP] = y.astype(BF16)

        cosv = cos_ref[rows, :]
        sinv = sin_ref[rows, :]
        for h in range(H_RET):
            lg = _ret_log_gamma(h)
            cols = slice(h * DK_RET, (h + 1) * DK_RET)
            xq = z_ref[rows, RQ0 + h * DK_RET:RQ0 + (h + 1) * DK_RET]
            xk = z_ref[rows, RK0 + h * DK_RET:RK0 + (h + 1) * DK_RET]
            q = (xq * cosv + pltpu.roll(xq, DK_RET // 2, axis=1) * sinv) * (DK_RET ** -0.5)
            k = xk * cosv + pltpu.roll(xk, DK_RET // 2, axis=1) * sinv
            v = z_ref[rows, RV0 + h * DK_RET:RV0 + (h + 1) * DK_RET].astype(BF16)
            decay = jnp.where(rel >= 0, jnp.exp(jnp.maximum(rel, 0.0) * lg), 0.0)
            q_dec = jnp.exp((rowf + 1.0) * lg)
            k_dec = jnp.exp((CHUNK - 1.0 - rowf) * lg)
            c_dec = math.exp(CHUNK * lg)
            qb16 = q.astype(BF16)
            att_r = _dot_nt(qb16, k.astype(BF16)) * decay
            s_old = str_ref[h]
            o = _dot(att_r.astype(BF16), v) + _dot(qb16, s_old.astype(BF16)) * q_dec
            str_ref[h] = c_dec * s_old + _dot_tn((k * k_dec).astype(BF16), v)
            gate = z_ref[rows, RG0 + h * DK_RET:RG0 + (h + 1) * DK_RET]
            y = _head_norm(o, DK_RET) * rng_ref[:, cols] * _silu(gate)
            mix_ref[rows, MR0 + h * DK_RET:MR0 + (h + 1) * DK_RET] = y.astype(BF16)

        return carry

    lax.fori_loop(0, tc // CHUNK, chunk_body, 0)

    cw = lcw_ref[...]
    xc = lcb_ref[...] + cw[0:1, :] * lxe_ref[pre - 3:pre - 3 + tc, :]
    xc = xc + cw[1:2, :] * lxe_ref[pre - 2:pre - 2 + tc, :]
    xc = xc + cw[2:3, :] * lxe_ref[pre - 1:pre - 1 + tc, :]
    xc = xc + cw[3:4, :] * lxe_ref[pre:pre + tc, :]
    a, u = _lru_gates(xc, wa_ref, ba_ref, wx_ref, bx_ref, lam_ref)
    a_ref[...] = a
    u_ref[...] = u

    def scan_body(t, h):
        h = a_ref[pl.ds(t, 1), :] * h + u_ref[pl.ds(t, 1), :]
        hs_ref[pl.ds(t, 1), :] = h
        return h

    h_fin = lax.fori_loop(0, tc, scan_body, h_ref[0:1, :], unroll=8)
    h_ref[0:1, :] = h_fin

    def out_body(c, carry):
        rows = pl.ds(pl.multiple_of(c * CHUNK, CHUNK), CHUNK)
        y = hs_ref[rows, :] * _gelu_tanh(z_ref[rows, LG0:LG0 + W_LRU])
        mix_ref[rows, ML0:ML0 + W_LRU] = y.astype(BF16)
        return carry

    lax.fori_loop(0, tc // CHUNK, out_body, 0)

    lxe_ref[0:pre, :] = lxe_ref[tc:tc + pre, :]

    @pl.when(t_id == nt - 1)
    def _():
        for h in range(H_GLA):
            sg_ref[h] = stg_ref[h].T[:DK_GLA, :DV_GLA]
        sr_ref[...] = str_ref[...]
        hl_ref[...] = h_fin
        lc_ref[...] = lxe_ref[tc + pre - (LRU_K - 1):tc + pre, :]


def _mixer_prompt_call(z, cos_t, sin_t, pw, *, layer):
    tc = MIX_TC
    nt = TP // tc

    def lspec(shape):
        return pl.BlockSpec((None,) + shape, lambda b, t: (layer,) + (0,) * len(shape))

    in_specs = [
        pl.BlockSpec((tc, NZ), lambda b, t: (b * nt + t, 0)),
        pl.BlockSpec((tc, LANES), lambda b, t: (t, 0)),
        pl.BlockSpec((tc, LANES), lambda b, t: (t, 0)),
        lspec((LANES, H_GLA * DKP)), lspec((1, H_GLA * DKP)), lspec((1, H_GLA * DVP)),
        lspec((LRU_K, W_LRU)), lspec((1, W_LRU)),
        lspec((W_LRU, W_LRU)), lspec((1, W_LRU)), lspec((W_LRU, W_LRU)), lspec((1, W_LRU)), lspec((1, W_LRU)),
        lspec((1, H_RET * DK_RET)),
    ]
    out_specs = [
        pl.BlockSpec((tc, NMIX), lambda b, t: (b * nt + t, 0)),
        pl.BlockSpec((None, H_GLA, DK_GLA, DV_GLA), lambda b, t: (b, 0, 0, 0)),
        pl.BlockSpec((None, H_RET, DK_RET, DK_RET), lambda b, t: (b, 0, 0, 0)),
        pl.BlockSpec((None, 1, W_LRU), lambda b, t: (b, 0, 0)),
        pl.BlockSpec((None, LRU_K - 1, W_LRU), lambda b, t: (b, 0, 0)),
    ]
    out_shape = [
        jax.ShapeDtypeStruct((BP * TP, NMIX), BF16),
        jax.ShapeDtypeStruct((BP, H_GLA, DK_GLA, DV_GLA), F32),
        jax.ShapeDtypeStruct((BP, H_RET, DK_RET, DK_RET), F32),
        jax.ShapeDtypeStruct((BP, 1, W_LRU), F32),
        jax.ShapeDtypeStruct((BP, LRU_K - 1, W_LRU), F32),
    ]
    scratch = [
        pltpu.VMEM((H_GLA, DVP, DKP), F32),
        pltpu.VMEM((H_RET, DK_RET, DK_RET), F32),
        pltpu.VMEM((SUBLANES, W_LRU), F32),
        pltpu.VMEM((SUBLANES + tc, W_LRU), F32),
        pltpu.VMEM((tc, W_LRU), F32), pltpu.VMEM((tc, W_LRU), F32), pltpu.VMEM((tc, W_LRU), F32),
        pltpu.VMEM((H_GLA, CHUNK, LANES), F32),
    ]
    return pl.pallas_call(
        _mixer_prompt_kernel,
        grid=(BP, nt),
        in_specs=in_specs,
        out_specs=out_specs,
        out_shape=out_shape,
        scratch_shapes=scratch,
        compiler_params=_cparams(("arbitrary", "arbitrary")),
        name="mixer_p",
    )(z, cos_t, sin_t, pw["w_alpha"], pw["b_alpha"], pw["gla_norm_g"], pw["lru_conv_w"], pw["lru_conv_b"],
      pw["lru_wa"], pw["lru_ba"], pw["lru_wx"], pw["lru_bx"], pw["lru_lam"], pw["ret_norm_g"])


MIX_BB = 8


def _mixer_sample_kernel(z_ref, cos_ref, sin_ref, wal_ref, bal_ref, gng_ref, lcw_ref, lcb_ref,
                         wa_ref, ba_ref, wx_ref, bx_ref, lam_ref, rng_ref,
                         sg0_ref, sr0_ref, h0_ref, lc0_ref,
                         mix_ref, sg_ref, sr_ref, hl_ref, lc_ref,
                         qb_s, kb_s, v_s, og_s, rq_s, rk_s, rv_s, or_s, sp_ref, et_ref):
    bb = MIX_BB
    nrow = TS * bb

    @pl.when(pl.program_id(0) == 0)
    def _():
        sp_ref[...] = jnp.zeros_like(sp_ref)
        et_ref[...] = jnp.zeros_like(et_ref)

    def zcols(c0, width):
        return z_ref[:, :, c0:c0 + width].reshape(nrow, width)

    def slab(x, t):
        return x[t * bb:(t + 1) * bb, :]

    zq = zcols(Q0, H_GLA * DKP) * (DK_GLA ** -0.5)
    zk = zcols(K0, H_GLA * DKP)
    zv = zcols(V0, H_GLA * DVP)
    la = _log_sigmoid(_dot(zcols(LR0, LANES).astype(BF16), wal_ref[...]) + bal_ref[...]) * (1.0 / GLA_TAU)
    bs = [slab(la, 0)]
    for t in range(1, TS):
        bs.append(bs[-1] + slab(la, t))
    b_last = bs[-1]
    e_last = jnp.exp(b_last)
    o_t = []
    for t in range(TS):
        acc = None
        for s in range(t + 1):
            w = slab(zq, t) * slab(zk, s) * jnp.exp(bs[t] - bs[s])
            parts = []
            for h in range(H_GLA):
                a_ts = jnp.sum(w[:, h * DKP:(h + 1) * DKP], axis=-1, keepdims=True)
                parts.append(a_ts * slab(zv, s)[:, h * DVP:(h + 1) * DVP])
            contrib = jnp.concatenate(parts, axis=1)
            acc = contrib if acc is None else acc + contrib
        o_t.append(acc)
    o_intra = jnp.concatenate(o_t, axis=0)
    b_all = jnp.concatenate(bs, axis=0)
    qb = zq * jnp.exp(b_all)
    kb = zk * jnp.exp(jnp.concatenate([b_last] * TS, axis=0) - b_all)
    for h in range(H_GLA):
        qb_s[h] = qb[:, h * DKP:(h + 1) * DKP]
        kb_s[h] = kb[:, h * DKP:(h + 1) * DKP]
        for p in range(DVP // LANES):
            v_s[h * (DVP // LANES) + p] = zv[:, h * DVP + p * LANES:h * DVP + (p + 1) * LANES]
        et_ref[h, 0:bb, :] = e_last[:, h * DKP:(h + 1) * DKP]
    e_cols = [et_ref[h].T for h in range(H_GLA)]

    for b in range(bb):
        seq = pl.ds(b, TS, stride=bb)
        for h in range(H_GLA):
            s0 = sg0_ref[b, h]
            sp_ref[0:DK_GLA, 0:DV_GLA] = s0
            q_bh = qb_s[h, seq, :]
            k_bh = kb_s[h, seq, :]
            v_bh = jnp.concatenate([v_s[h * (DVP // LANES) + p, seq, :] for p in range(DVP // LANES)], axis=1)
            o_bh = _dot(q_bh, sp_ref[...])
            for p in range(DVP // LANES):
                og_s[h * (DVP // LANES) + p, seq, :] = o_bh[:, p * LANES:(p + 1) * LANES]
            upd = _dot_tn(k_bh, v_bh)
            s_new = e_cols[h][:, b:b + 1] * sp_ref[...] + upd
            sg_ref[b, h] = s_new[:DK_GLA, :DV_GLA]

    og = jnp.concatenate([og_s[i] for i in range(H_GLA * DVP // LANES)], axis=1) + o_intra
    for h in range(H_GLA):
        cols = slice(h * DVP, (h + 1) * DVP)
        y = _head_norm(og[:, cols], DV_GLA) * gng_ref[:, cols] * _silu(zcols(GG0 + h * DVP, DVP))
        mix_ref[:, :, MG0 + h * DVP:MG0 + (h + 1) * DVP] = y.reshape(TS, bb, DVP)

    cosv = jnp.concatenate([jnp.broadcast_to(cos_ref[t:t + 1, :], (bb, LANES)) for t in range(TS)], axis=0)
    sinv = jnp.concatenate([jnp.broadcast_to(sin_ref[t:t + 1, :], (bb, LANES)) for t in range(TS)], axis=0)
    trow = lax.broadcasted_iota(jnp.int32, (TS, LANES), 0).astype(F32)
    r_intra = []
    for h in range(H_RET):
        lg = _ret_log_gamma(h)
        xq = zcols(RQ0 + h * DK_RET, DK_RET)
        xk = zcols(RK0 + h * DK_RET, DK_RET)
        q = (xq * cosv + pltpu.roll(xq, DK_RET // 2, axis=1) * sinv) * (DK_RET ** -0.5)
        k = xk * cosv + pltpu.roll(xk, DK_RET // 2, axis=1) * sinv
        v = zcols(RV0 + h * DK_RET, DK_RET)
        rq_s[h] = q
        rk_s[h] = k
        rv_s[h] = v
        outs = []
        for t in range(TS):
            acc = None
            for s in range(t + 1):
                a_ts = jnp.sum(slab(q, t) * slab(k, s), axis=-1, keepdims=True) * math.exp((t - s) * lg)
                contrib = a_ts * slab(v, s)
                acc = contrib if acc is None else acc + contrib
            outs.append(acc)
        r_intra.append(jnp.concatenate(outs, axis=0))

    for b in range(bb):
        seq = pl.ds(b, TS, stride=bb)
        for h in range(H_RET):
            lg = _ret_log_gamma(h)
            q_dec = jnp.exp((trow + 1.0) * lg)
            k_dec = jnp.exp((TS - 1.0 - trow) * lg)
            s0 = sr0_ref[b, h]
            q_bh = rq_s[h, seq, :]
            k_bh = rk_s[h, seq, :]
            v_bh = rv_s[h, seq, :]
            or_s[h, seq, :] = _dot(q_bh, s0) * q_dec
            sr_ref[b, h] = math.exp(TS * lg) * s0 + _dot_tn(k_bh * k_dec, v_bh)

    for h in range(H_RET):
        cols = slice(h * DK_RET, (h + 1) * DK_RET)
        o = or_s[h] + r_intra[h]
        y = _head_norm(o, DK_RET) * rng_ref[:, cols] * _silu(zcols(RG0 + h * DK_RET, DK_RET))
        mix_ref[:, :, MR0 + h * DK_RET:MR0 + (h + 1) * DK_RET] = y.reshape(TS, bb, DK_RET)

    lx = zcols(LX0, W_LRU)
    xe = [lc0_ref[:, k * W_LRU:(k + 1) * W_LRU] for k in range(LRU_K - 1)] + [slab(lx, t) for t in range(TS)]
    cw = lcw_ref[...]
    xc = []
    for t in range(TS):
        acc = lcb_ref[...] + cw[0:1, :] * xe[t]
        for k in range(1, LRU_K):
            acc = acc + cw[k:k + 1, :] * xe[t + k]
        xc.append(acc)
    a, u = _lru_gates(jnp.concatenate(xc, axis=0), wa_ref, ba_ref, wx_ref, bx_ref, lam_ref)
    h = h0_ref[...]
    hs = []
    for t in range(TS):
        h = slab(a, t) * h + slab(u, t)
        hs.append(h)
    y = jnp.concatenate(hs, axis=0) * _gelu_tanh(zcols(LG0, W_LRU))
    mix_ref[:, :, ML0:ML0 + W_LRU] = y.reshape(TS, bb, W_LRU)
    hl_ref[...] = h
    for k in range(LRU_K - 1):
        lc_ref[:, k * W_LRU:(k + 1) * W_LRU] = xe[TS + k]


def _mixer_sample_call(z, cos_t, sin_t, pw, st_gla, st_ret, st_lru, st_lconv, *, layer):
    bb = MIX_BB

    def lspec(shape):
        return pl.BlockSpec((None,) + shape, lambda j: (layer,) + (0,) * len(shape))

    in_specs = [
        pl.BlockSpec((TS, bb, NZ), lambda j: (0, j, 0)),
        pl.BlockSpec((TS, LANES), lambda j: (0, 0)),
        pl.BlockSpec((TS, LANES), lambda j: (0, 0)),
        lspec((LANES, H_GLA * DKP)), lspec((1, H_GLA * DKP)), lspec((1, H_GLA * DVP)),
        lspec((LRU_K, W_LRU)), lspec((1, W_LRU)),
        lspec((W_LRU, W_LRU)), lspec((1, W_LRU)), lspec((W_LRU, W_LRU)), lspec((1, W_LRU)), lspec((1, W_LRU)),
        lspec((1, H_RET * DK_RET)),
        pl.BlockSpec((None, bb, H_GLA, DK_GLA, DV_GLA), lambda j: (layer, j, 0, 0, 0)),
        pl.BlockSpec((None, bb, H_RET, DK_RET, DK_RET), lambda j: (layer, j, 0, 0, 0)),
        pl.BlockSpec((None, bb, W_LRU), lambda j: (layer, j, 0)),
        pl.BlockSpec((None, bb, (LRU_K - 1) * W_LRU), lambda j: (layer, j, 0)),
    ]
    out_specs = [
        pl.BlockSpec((TS, bb, NMIX), lambda j: (0, j, 0)),
        pl.BlockSpec((bb, H_GLA, DK_GLA, DV_GLA), lambda j: (j, 0, 0, 0)),
        pl.BlockSpec((bb, H_RET, DK_RET, DK_RET), lambda j: (j, 0, 0, 0)),
        pl.BlockSpec((bb, W_LRU), lambda j: (j, 0)),
        pl.BlockSpec((bb, (LRU_K - 1) * W_LRU), lambda j: (j, 0)),
    ]
    out_shape = [
        jax.ShapeDtypeStruct((TS, BS, NMIX), F32),
        jax.ShapeDtypeStruct((BS, H_GLA, DK_GLA, DV_GLA), F32),
        jax.ShapeDtypeStruct((BS, H_RET, DK_RET, DK_RET), F32),
        jax.ShapeDtypeStruct((BS, W_LRU), F32),
        jax.ShapeDtypeStruct((BS, (LRU_K - 1) * W_LRU), F32),
    ]
    nrow = TS * bb
    scratch = [
        pltpu.VMEM((H_GLA, nrow, LANES), F32), pltpu.VMEM((H_GLA, nrow, LANES), F32),
        pltpu.VMEM((H_GLA * DVP // LANES, nrow, LANES), F32), pltpu.VMEM((H_GLA * DVP // LANES, nrow, LANES), F32),
        pltpu.VMEM((H_RET, nrow, LANES), F32), pltpu.VMEM((H_RET, nrow, LANES), F32),
        pltpu.VMEM((H_RET, nrow, LANES), F32), pltpu.VMEM((H_RET, nrow, LANES), F32),
        pltpu.VMEM((DKP, DVP), F32),
        pltpu.VMEM((H_GLA, LANES, LANES), F32),
    ]
    return pl.pallas_call(
        _mixer_sample_kernel,
        grid=(BS // bb,),
        in_specs=in_specs,
        out_specs=out_specs,
        out_shape=out_shape,
        scratch_shapes=scratch,
        compiler_params=_cparams(("arbitrary",)),
        name="mixer_s",
    )(z, cos_t, sin_t, pw["w_alpha"], pw["b_alpha"], pw["gla_norm_g"], pw["lru_conv_w"], pw["lru_conv_b"],
      pw["lru_wa"], pw["lru_ba"], pw["lru_wx"], pw["lru_bx"], pw["lru_lam"], pw["ret_norm_g"],
      st_gla, st_ret, st_lru, st_lconv)


def _pad_heads(w, heads, d, dp):
    lead = w.shape[:-1]
    w = w.reshape(lead + (heads, d))
    w = jnp.pad(w, [(0, 0)] * len(lead) + [(0, 0), (0, dp - d)])
    return w.reshape(lead + (heads * dp,))


def _split_cols(w, sizes):
    out, off = [], 0
    for s in sizes:
        out.append(w[..., off:off + s])
        off += s
    return out


def _rope_tables(start, length):
    half = DK_RET // 2
    freqs = ROPE_BASE ** (-jnp.arange(half, dtype=F32) / half)
    pos = start + jnp.arange(length, dtype=jnp.int32)
    ang = pos.astype(F32)[:, None] * freqs[None, :]
    cos, sin = jnp.cos(ang), jnp.sin(ang)
    return jnp.concatenate([cos, cos], axis=1), jnp.concatenate([-sin, sin], axis=1)


def _prep_weights(w_in, gla_w_alpha, gla_b_alpha, gla_norm_g, lru_conv_w, lru_conv_b, lru_w_a, lru_b_a,
                  lru_w_x, lru_b_x, lru_lambda, ret_norm_g, w_out, ffn_w_up, ffn_conv_w, ffn_conv_b, ffn_w_down):
    w_gla, w_ret = H_GLA * DV_GLA, H_RET * DK_RET
    sizes = (H_GLA * DK_GLA, H_GLA * DK_GLA, w_gla, GLA_RANK, w_gla, W_LRU, W_LRU, w_ret, w_ret, w_ret, w_ret)
    gq, gk, gv, glr, gg, lx, lg, rq, rk, rv, rg = _split_cols(w_in, sizes)
    w_in_p = jnp.concatenate([
        _pad_heads(gq, H_GLA, DK_GLA, DKP), _pad_heads(gk, H_GLA, DK_GLA, DKP), _pad_heads(gv, H_GLA, DV_GLA, DVP),
        jnp.pad(glr, ((0, 0), (0, 0), (0, LANES - GLA_RANK))), _pad_heads(gg, H_GLA, DV_GLA, DVP),
        lx, lg, rq, rk, rv, rg, jnp.zeros((DEPTH, D, NZ - NZ_USED), F32)], axis=-1).astype(BF16)
    w_alpha = jnp.pad(_pad_heads(gla_w_alpha, H_GLA, DK_GLA, DKP), ((0, 0), (0, LANES - GLA_RANK), (0, 0))).astype(BF16)
    eye = jnp.eye(H_LRU, dtype=F32)

    def block_diag(w):
        return jnp.einsum("lhij,hg->lhigj", w, eye).reshape(DEPTH, W_LRU, W_LRU).astype(BF16)

    wo_g, wo_l, wo_r = _split_cols(jnp.swapaxes(w_out, 1, 2), (w_gla, W_LRU, w_ret))
    w_out_p = jnp.swapaxes(jnp.concatenate([_pad_heads(wo_g, H_GLA, DV_GLA, DVP), wo_l, wo_r], axis=-1), 1, 2)
    return dict(
        w_in=w_in_p,
        w_alpha=w_alpha,
        b_alpha=_pad_heads(gla_b_alpha, H_GLA, DK_GLA, DKP)[:, None, :],
        gla_norm_g=_pad_heads(gla_norm_g, H_GLA, DV_GLA, DVP)[:, None, :],
        lru_conv_w=lru_conv_w,
        lru_conv_b=lru_conv_b[:, None, :],
        lru_wa=block_diag(lru_w_a), lru_ba=lru_b_a[:, None, :],
        lru_wx=block_diag(lru_w_x), lru_bx=lru_b_x[:, None, :],
        lru_lam=lru_lambda[:, None, :],
        ret_norm_g=ret_norm_g[:, None, :],
        w_out=w_out_p.astype(BF16),
        w_up=ffn_w_up.astype(BF16),
        ffn_cw=ffn_conv_w,
        ffn_cb=ffn_conv_b[:, None, :],
        w_down=ffn_w_down.astype(BF16),
    )


def _run_group(x, mod, pw, norm1_g, norm2_g, tables, states, *, prompt):
    tm = 1024
    tn_in, tn_out, tn_down = 1024, 1024, 512
    outs = []
    for l in range(DEPTH):
        z = _proj_call(x, norm1_g, mod, pw["w_in"], layer=l, sc_chunk=1, sh_chunk=0, prompt=prompt, tm=tm, tn=tn_in)
        if prompt:
            mixed, s_gla, s_ret, s_lru, s_lconv = _mixer_prompt_call(z, *tables, pw, layer=l)
            s_lru = s_lru.reshape(BP, W_LRU)
        else:
            st_gla, st_ret, st_lru, st_lconv, _ = states
            mixed, s_gla, s_ret, s_lru, s_lconv = _mixer_sample_call(
                z.reshape(TS, BS, NZ), *tables, pw, st_gla, st_ret, st_lru, st_lconv, layer=l)
            mixed = mixed.reshape(TS * BS, NMIX)
            s_lconv = s_lconv.reshape(BS, LRU_K - 1, W_LRU)
        x = _resid_call(mixed, pw["w_out"], x, mod, layer=l, g_chunk=2, prompt=prompt, tm=tm, tn=tn_out,
                        name="out_proj_p" if prompt else "out_proj_s")
        act, s_fg, s_fv = _ffn_up_call(x, norm2_g, mod, pw["w_up"], pw["ffn_cw"], pw["ffn_cb"],
                                       None if prompt else states[4], layer=l, prompt=prompt, tm=tm)
        x = _resid_call(act, pw["w_down"], x, mod, layer=l, g_chunk=5, prompt=prompt, tm=tm, tn=tn_down,
                        name="ffn_down_p" if prompt else "ffn_down_s")
        s_fconv = jnp.concatenate([s_fg, s_fv], axis=-1)
        if prompt:
            s_fconv = s_fconv[TP // tm - 1::TP // tm]
        else:
            s_fconv = jnp.swapaxes(s_fconv, 0, 1)
        outs.append((s_gla, s_ret, s_lru, s_lconv, s_fconv))
    return x, [jnp.stack(o) for o in zip(*outs)]


def kernel(x_prompt, x_sample, state_gla, state_ret, state_lru, state_lru_conv, state_ffn_conv, c_prompt, c_sample,
           norm1_g, norm2_g, final_g, w_ada, b_ada, w_in, gla_w_alpha, gla_b_alpha, gla_norm_g, lru_conv_w,
           lru_conv_b, lru_w_a, lru_b_a, lru_w_x, lru_b_x, lru_lambda, ret_norm_g, w_out, ffn_w_up, ffn_conv_w,
           ffn_conv_b, ffn_w_down):
    pw = _prep_weights(w_in, gla_w_alpha, gla_b_alpha, gla_norm_g, lru_conv_w, lru_conv_b, lru_w_a, lru_b_a,
                       lru_w_x, lru_b_x, lru_lambda, ret_norm_g, w_out, ffn_w_up, ffn_conv_w, ffn_conv_b, ffn_w_down)
    cp8 = jnp.pad(c_prompt, ((0, SUBLANES - BP), (0, 0)))
    mod_p, mod_s = _mod_call(cp8, c_sample, w_ada, b_ada[:, None, :])
    n1 = norm1_g[:, None, :]
    n2 = norm2_g[:, None, :]

    xp = x_prompt.reshape(BP * TP, D)
    xs = jnp.swapaxes(x_sample, 0, 1).reshape(TS * BS, D)
    st_s = (state_gla, state_ret, state_lru, state_lru_conv.reshape(DEPTH, BS, (LRU_K - 1) * W_LRU),
            jnp.swapaxes(state_ffn_conv, 1, 2))

    xp, outs_p = _run_group(xp, mod_p, pw, n1, n2, _rope_tables(0, TP), None, prompt=True)
    xs, outs_s = _run_group(xs, mod_s, pw, n1, n2, _rope_tables(PAST, TS), st_s, prompt=False)

    fg = final_g[None, :]
    y_p = _final_call(xp, fg).reshape(BP, TP, D)
    y_s = jnp.swapaxes(_final_call(xs, fg).reshape(TS, BS, D), 0, 1)
    return (y_p, y_s, *outs_p, *outs_s)
```

```python
import functools
import math

import jax
import jax.numpy as jnp
from jax import lax
from jax.experimental import pallas as pl
from jax.experimental.pallas import tpu as pltpu

F32 = jnp.float32
BF16 = jnp.bfloat16

D = 2048
DEPTH = 4
BP, TP = 4, 2048
BS, TS = 128, 8
PAST = 16384
H_GLA, DK_GLA, DV_GLA = 4, 96, 192
GLA_RANK = 16
GLA_TAU = 16.0
W_LRU, H_LRU, BLK_LRU = 768, 8, 96
LRU_K = 4
LRU_C = 8.0
H_RET, DK_RET = 4, 128
ROPE_BASE = 10000.0
D_FF = 5632
FFN_K = 3
CHUNK = 64
EPS = 1e-6

LANES = 128
SUBLANES = 8
DKP, DVP = 128, 256

Q0 = 0
K0 = Q0 + H_GLA * DKP
V0 = K0 + H_GLA * DKP
LR0 = V0 + H_GLA * DVP
GG0 = LR0 + LANES
LX0 = GG0 + H_GLA * DVP
LG0 = LX0 + W_LRU
RQ0 = LG0 + W_LRU
RK0 = RQ0 + H_RET * DK_RET
RV0 = RK0 + H_RET * DK_RET
RG0 = RV0 + H_RET * DK_RET
NZ_USED = RG0 + H_RET * DK_RET
NZ = -(-NZ_USED // 1024) * 1024
MG0 = 0
ML0 = H_GLA * DVP
MR0 = ML0 + W_LRU
NMIX = MR0 + H_RET * DK_RET

SUB = 16
NEG = -1e30
FF_TN = 512
FF_NJ = D_FF // FF_TN

VMEM_LIMIT = 56 * 1024 * 1024


def _cparams(sem, flags=None):
    return pltpu.CompilerParams(dimension_semantics=sem, vmem_limit_bytes=VMEM_LIMIT, flags=flags)


def _sigmoid(x):
    return 1.0 / (1.0 + jnp.exp(-x))


def _silu(x):
    return x * _sigmoid(x)


def _gelu_tanh(x):
    c = math.sqrt(2.0 / math.pi)
    return 0.5 * x * (1.0 + jnp.tanh(c * (x + 0.044715 * (x * x * x))))


def _log_sigmoid(x):
    return jnp.minimum(x, 0.0) - jnp.log1p(jnp.exp(-jnp.abs(x)))


def _softplus(x):
    return jnp.maximum(x, 0.0) + jnp.log1p(jnp.exp(-jnp.abs(x)))


def _dot(a, b):
    return jnp.dot(a, b, preferred_element_type=F32)


def _dot_nt(a, b):
    return lax.dot_general(a, b, (((1,), (1,)), ((), ())), preferred_element_type=F32)


def _dot_tn(a, b):
    return lax.dot_general(a, b, (((0,), (0,)), ((), ())), preferred_element_type=F32)


def _norm_mod(x, g, sc, sh):
    ms = jnp.mean(x * x, axis=-1, keepdims=True)
    return (x * lax.rsqrt(ms + EPS)) * g * (1.0 + sc) + sh


def _head_norm(o, width):
    ms = jnp.sum(o * o, axis=-1, keepdims=True) * (1.0 / width)
    return o * lax.rsqrt(ms + EPS)


def _mod_kernel(cp_ref, cs_ref, w_ref, b_ref, op_ref, os_ref):
    w = w_ref[...].astype(BF16)
    b = b_ref[...]
    op_ref[...] = _dot(_silu(cp_ref[...]).astype(BF16), w) + b
    os_ref[...] = _dot(_silu(cs_ref[...]).astype(BF16), w) + b


def _mod_call(cp8, cs, w_ada, b_ada):
    tn = 1024
    n = 6 * D
    return pl.pallas_call(
        _mod_kernel,
        grid=(DEPTH, n // tn),
        in_specs=[
            pl.BlockSpec((SUBLANES, D), lambda l, j: (0, 0)),
            pl.BlockSpec((BS, D), lambda l, j: (0, 0)),
            pl.BlockSpec((None, D, tn), lambda l, j: (l, 0, j)),
            pl.BlockSpec((None, 1, tn), lambda l, j: (l, 0, j)),
        ],
        out_specs=[
            pl.BlockSpec((None, SUBLANES, tn), lambda l, j: (l, 0, j)),
            pl.BlockSpec((None, BS, tn), lambda l, j: (l, 0, j)),
        ],
        out_shape=[
            jax.ShapeDtypeStruct((DEPTH, SUBLANES, n), F32),
            jax.ShapeDtypeStruct((DEPTH, BS, n), F32),
        ],
        compiler_params=_cparams(("arbitrary", "arbitrary")),
        name="adaln_mod",
    )(cp8, cs, w_ada, b_ada)


ROW_CHUNK = 128


def _fill_hn(x_ref, g_ref, sc_ref, sh_ref, hn_ref, *, tile, tm, prompt, seq_tiles):
    g = g_ref[...]
    if prompt:
        b = tile // seq_tiles
        sc = sc_ref[pl.ds(b, 1), :]
        sh = sh_ref[pl.ds(b, 1), :]

    def body(r, carry):
        rows = pl.ds(pl.multiple_of(r * ROW_CHUNK, ROW_CHUNK), ROW_CHUNK)
        if prompt:
            y = _norm_mod(x_ref[rows, :], g, sc, sh)
        else:
            y = _norm_mod(x_ref[rows, :], g, sc_ref[...], sh_ref[...])
        hn_ref[rows, :] = y.astype(BF16)
        return carry

    lax.fori_loop(0, tm // ROW_CHUNK, body, 0)


def _mod_specs(layer, chunks, prompt, width, col_of):
    rows = SUBLANES if prompt else BS
    per = D // width
    return [pl.BlockSpec((None, rows, width), (lambda i, j, c=c: (layer, 0, c * per + col_of(i, j)))) for c in chunks]


def _proj_kernel(x_ref, g_ref, sc_ref, sh_ref, w_ref, o_ref, hn_ref, *, tm, prompt, seq_tiles):
    @pl.when(pl.program_id(1) == 0)
    def _():
        _fill_hn(x_ref, g_ref, sc_ref, sh_ref, hn_ref, tile=pl.program_id(0), tm=tm, prompt=prompt,
                 seq_tiles=seq_tiles)

    o_ref[...] = _dot(hn_ref[...], w_ref[...])


def _proj_call(x, norm_g, mod, w, *, layer, sc_chunk, sh_chunk, prompt, tm, tn):
    m = x.shape[0]
    n = w.shape[-1]
    kern = functools.partial(_proj_kernel, tm=tm, prompt=prompt, seq_tiles=TP // tm if prompt else 1)
    sc_spec, sh_spec = _mod_specs(layer, (sc_chunk, sh_chunk), prompt, D, lambda i, j: 0)
    return pl.pallas_call(
        kern,
        grid=(m // tm, n // tn),
        in_specs=[
            pl.BlockSpec((tm, D), lambda i, j: (i, 0)),
            pl.BlockSpec((None, 1, D), lambda i, j: (layer, 0, 0)),
            sc_spec,
            sh_spec,
            pl.BlockSpec((None, D, tn), lambda i, j: (layer, 0, j)),
        ],
        out_specs=pl.BlockSpec((tm, tn), lambda i, j: (i, j)),
        out_shape=jax.ShapeDtypeStruct((m, n), F32),
        scratch_shapes=[pltpu.VMEM((tm, D), BF16)],
        compiler_params=_cparams(("arbitrary", "arbitrary")),
        name="in_proj_p" if prompt else "in_proj_s",
    )(x, norm_g, mod, mod, w)


def _resid_kernel(a_ref, w_ref, x_ref, g_ref, o_ref, *, tm, prompt, seq_tiles):
    acc = _dot(a_ref[...].astype(BF16), w_ref[...])
    if prompt:
        b = pl.program_id(0) // seq_tiles
        o_ref[...] = x_ref[...] + g_ref[pl.ds(b, 1), :] * acc
    else:
        g = g_ref[...]
        for t in range(tm // BS):
            rows = slice(t * BS, (t + 1) * BS)
            o_ref[rows, :] = x_ref[rows, :] + g * acc[rows, :]


def _resid_call(a, w, x, mod, *, layer, g_chunk, prompt, tm, tn, name):
    m, k = a.shape
    kern = functools.partial(_resid_kernel, tm=tm, prompt=prompt, seq_tiles=TP // tm if prompt else 1)
    (g_spec,) = _mod_specs(layer, (g_chunk,), prompt, tn, lambda i, j: j)
    return pl.pallas_call(
        kern,
        grid=(m // tm, D // tn),
        in_specs=[
            pl.BlockSpec((tm, k), lambda i, j: (i, 0)),
            pl.BlockSpec((None, k, tn), lambda i, j: (layer, 0, j)),
            pl.BlockSpec((tm, tn), lambda i, j: (i, j)),
            g_spec,
        ],
        out_specs=pl.BlockSpec((tm, tn), lambda i, j: (i, j)),
        out_shape=jax.ShapeDtypeStruct((m, D), F32),
        compiler_params=_cparams(("arbitrary", "arbitrary")),
        name=name,
    )(a, w, x, mod)


def _ffn_up_kernel(*refs, tm, prompt, seq_tiles, pre, shift, n_steps):
    (x_ref, g_ref, sc_ref, sh_ref, wg_ref, wv_ref, cwg_ref, cwv_ref, cbg_ref, cbv_ref) = refs[:10]
    if prompt:
        act_ref, stg_ref, stv_ref, hn_ref, ue_a, ue_b, carry_ref = refs[10:]
    else:
        s0g_ref, s1g_ref, s0v_ref, s1v_ref, act_ref, stg_ref, stv_ref, hn_ref, ue_a, ue_b = refs[10:]
    s = pl.program_id(0)
    s_mm = jnp.minimum(s, n_steps - 1)
    s_ep = jnp.maximum(s - 1, 0)
    gate_cols = slice(0, FF_TN)
    val_cols = slice(FF_TN, 2 * FF_TN)

    @pl.when(s == 0)
    def _():
        ue_b[...] = jnp.zeros_like(ue_b)

    @pl.when(jnp.logical_and(s_mm % FF_NJ == 0, s < n_steps))
    def _():
        _fill_hn(x_ref, g_ref, sc_ref, sh_ref, hn_ref, tile=s_mm // FF_NJ, tm=tm, prompt=prompt,
                 seq_tiles=seq_tiles)

    def step(ue_mm, ue_ep):
        if prompt:
            first = ((s_ep // FF_NJ) % seq_tiles) == 0

            @pl.when(first)
            def _():
                ue_ep[0:pre, :] = jnp.zeros((pre, 2 * FF_TN), F32)

            @pl.when(jnp.logical_not(first))
            def _():
                ue_ep[0:pre, :] = carry_ref[s_ep % FF_NJ]

        if not prompt:
            ue_ep[0:BS, gate_cols] = s0g_ref[...]
            ue_ep[0:BS, val_cols] = s0v_ref[...]
            ue_ep[BS:2 * BS, gate_cols] = s1g_ref[...]
            ue_ep[BS:2 * BS, val_cols] = s1v_ref[...]

        def conv(base, cols, cw, cb):
            u = cb + cw[0:1, :] * ue_ep[base + pre - 2 * shift:base + pre - 2 * shift + rc, cols]
            u = u + cw[1:2, :] * ue_ep[base + pre - shift:base + pre - shift + rc, cols]
            return u + cw[2:3, :] * ue_ep[base + pre:base + pre + rc, cols]

        rc = 16
        cwg, cwv, cbg, cbv = cwg_ref[...], cwv_ref[...], cbg_ref[...], cbv_ref[...]
        for r in range(tm // rc):
            gate = conv(r * rc, gate_cols, cwg, cbg)
            val = conv(r * rc, val_cols, cwv, cbv)
            act_ref[r * rc:(r + 1) * rc, :] = (_silu(gate) * val).astype(BF16)
        if prompt:
            carry_ref[s_ep % FF_NJ] = ue_ep[tm:tm + pre, :]
            stg_ref[...] = ue_ep[pre + tm - (FFN_K - 1):pre + tm, gate_cols]
            stv_ref[...] = ue_ep[pre + tm - (FFN_K - 1):pre + tm, val_cols]
        else:
            for k in range(FFN_K - 1):
                rows = slice(pre + tm - (FFN_K - 1 - k) * BS, pre + tm - (FFN_K - 2 - k) * BS)
                stg_ref[k] = ue_ep[rows, gate_cols]
                stv_ref[k] = ue_ep[rows, val_cols]
        hn = hn_ref[...]
        ue_mm[pre:pre + tm, gate_cols] = _dot(hn, wg_ref[...])
        ue_mm[pre:pre + tm, val_cols] = _dot(hn, wv_ref[...])

    @pl.when(s % 2 == 0)
    def _():
        step(ue_a, ue_b)

    @pl.when(s % 2 == 1)
    def _():
        step(ue_b, ue_a)


def _ffn_up_call(x, norm_g, mod, w, cw, cb, st_in, *, layer, prompt, tm):
    m = x.shape[0]
    seq_tiles = TP // tm if prompt else 1
    pre = SUBLANES if prompt else (FFN_K - 1) * BS
    shift = 1 if prompt else BS
    n_steps = (m // tm) * FF_NJ
    kern = functools.partial(_ffn_up_kernel, tm=tm, prompt=prompt, seq_tiles=seq_tiles, pre=pre, shift=shift,
                             n_steps=n_steps)

    def mm_blk(s):
        return jnp.minimum(s, n_steps - 1)

    def ep_blk(s):
        return jnp.maximum(s - 1, 0)

    mod_rows = SUBLANES if prompt else BS
    in_specs = [
        pl.BlockSpec((tm, D), lambda s: (mm_blk(s) // FF_NJ, 0)),
        pl.BlockSpec((None, 1, D), lambda s: (layer, 0, 0)),
        pl.BlockSpec((None, mod_rows, D), lambda s: (layer, 0, 4)),
        pl.BlockSpec((None, mod_rows, D), lambda s: (layer, 0, 3)),
        pl.BlockSpec((None, D, FF_TN), lambda s: (layer, 0, mm_blk(s) % FF_NJ)),
        pl.BlockSpec((None, D, FF_TN), lambda s: (layer, 0, FF_NJ + mm_blk(s) % FF_NJ)),
        pl.BlockSpec((None, FFN_K, FF_TN), lambda s: (layer, 0, ep_blk(s) % FF_NJ)),
        pl.BlockSpec((None, FFN_K, FF_TN), lambda s: (layer, 0, FF_NJ + ep_blk(s) % FF_NJ)),
        pl.BlockSpec((None, 1, FF_TN), lambda s: (layer, 0, ep_blk(s) % FF_NJ)),
        pl.BlockSpec((None, 1, FF_TN), lambda s: (layer, 0, FF_NJ + ep_blk(s) % FF_NJ)),
    ]
    args = [x, norm_g, mod, mod, w, w, cw, cw, cb, cb]
    scratch = [pltpu.VMEM((tm, D), BF16), pltpu.VMEM((pre + tm, 2 * FF_TN), F32),
               pltpu.VMEM((pre + tm, 2 * FF_TN), F32)]
    if prompt:
        st_spec = pl.BlockSpec((None, FFN_K - 1, FF_TN), lambda s: (ep_blk(s) // FF_NJ, 0, ep_blk(s) % FF_NJ))
        st_shape = jax.ShapeDtypeStruct((m // tm, FFN_K - 1, D_FF), F32)
        scratch.append(pltpu.VMEM((FF_NJ, pre, 2 * FF_TN), F32))
    else:
        for half in range(2):
            in_specs += [pl.BlockSpec((None, None, BS, FF_TN),
                                      lambda s, k=k, half=half: (layer, k, 0, half * FF_NJ + ep_blk(s) % FF_NJ))
                         for k in range(FFN_K - 1)]
            args += [st_in] * (FFN_K - 1)
        st_spec = pl.BlockSpec((FFN_K - 1, BS, FF_TN), lambda s: (0, 0, ep_blk(s) % FF_NJ))
        st_shape = jax.ShapeDtypeStruct((FFN_K - 1, BS, D_FF), F32)
    return pl.pallas_call(
        kern,
        grid=(n_steps + 1,),
        in_specs=in_specs,
        out_specs=[pl.BlockSpec((tm, FF_TN), lambda s: (ep_blk(s) // FF_NJ, ep_blk(s) % FF_NJ)), st_spec, st_spec],
        out_shape=[jax.ShapeDtypeStruct((m, D_FF), BF16), st_shape, st_shape],
        scratch_shapes=scratch,
        compiler_params=_cparams(("arbitrary",)),
        name="ffn_up_p" if prompt else "ffn_up_s",
    )(*args)


def _final_kernel(x_ref, g_ref, o_ref):
    x = x_ref[...]
    ms = jnp.mean(x * x, axis=-1, keepdims=True)
    o_ref[...] = (x * lax.rsqrt(ms + EPS)) * g_ref[...]


def _final_call(x, g):
    m = x.shape[0]
    tm = 256
    return pl.pallas_call(
        _final_kernel,
        grid=(m // tm,),
        in_specs=[pl.BlockSpec((tm, D), lambda i: (i, 0)), pl.BlockSpec((1, D), lambda i: (0, 0))],
        out_specs=pl.BlockSpec((tm, D), lambda i: (i, 0)),
        out_shape=jax.ShapeDtypeStruct((m, D), F32),
        compiler_params=_cparams(("arbitrary",)),
        name="final_norm",
    )(x, g)


def _split3(x):
    x1 = x.astype(BF16)
    r = x - x1.astype(F32)
    x2 = r.astype(BF16)
    r = r - x2.astype(F32)
    return x1, x2, r.astype(BF16)


def _ret_log_gamma(h):
    return math.log1p(-(2.0 ** (-5.0 - h)))


def _lru_gates(xc, wa_ref, ba_ref, wx_ref, bx_ref, lam_ref):
    xb = xc.astype(BF16)
    r = _sigmoid(_dot(xb, wa_ref[...]) + ba_ref[...])
    i = _sigmoid(_dot(xb, wx_ref[...]) + bx_ref[...])
    log_a = (-LRU_C * r) * _softplus(-lam_ref[...])
    a = jnp.exp(log_a)
    u = jnp.sqrt(-jnp.tanh(log_a) * (a * a + 1.0)) * (i * xc)
    return a, u


MIX_TC = 256


def _mixer_prompt_kernel(z_ref, cos_ref, sin_ref, wal_ref, bal_ref, gng_ref, lcw_ref, lcb_ref,
                         wa_ref, ba_ref, wx_ref, bx_ref, lam_ref, rng_ref,
                         mix_ref, sg_ref, sr_ref, hl_ref, lc_ref,
                         stg_ref, str_ref, h_ref, lxe_ref, a_ref, u_ref, hs_ref, att_s):
    t_id = pl.program_id(1)
    nt = pl.num_programs(1)
    tc = MIX_TC
    pre = SUBLANES

    @pl.when(t_id == 0)
    def _():
        stg_ref[...] = jnp.zeros_like(stg_ref)
        str_ref[...] = jnp.zeros_like(str_ref)
        h_ref[...] = jnp.zeros_like(h_ref)
        lxe_ref[0:pre, :] = jnp.zeros((pre, W_LRU), F32)

    lxe_ref[pre:pre + tc, :] = z_ref[:, LX0:LX0 + W_LRU]

    row_c = lax.broadcasted_iota(jnp.int32, (CHUNK, CHUNK), 0)
    col_c = lax.broadcasted_iota(jnp.int32, (CHUNK, CHUNK), 1)
    tri = (col_c <= row_c).astype(BF16)
    rel = (row_c - col_c).astype(F32)
    rowf = lax.broadcasted_iota(jnp.int32, (CHUNK, LANES), 0).astype(F32)
    row1 = lax.broadcasted_iota(jnp.int32, (CHUNK, 1), 0)
    sub_row = lax.broadcasted_iota(jnp.int32, (SUBLANES, 1), 0)
    lane_id = lax.broadcasted_iota(jnp.int32, (SUBLANES, LANES), 1)

    def chunk_body(c, carry):
        r0 = pl.multiple_of(c * CHUNK, CHUNK)
        rows = pl.ds(r0, CHUNK)

        zq = z_ref[rows, Q0:Q0 + H_GLA * DKP] * (DK_GLA ** -0.5)
        zk = z_ref[rows, K0:K0 + H_GLA * DKP]
        lr = z_ref[rows, LR0:LR0 + LANES].astype(BF16)
        la = _log_sigmoid(_dot(lr, wal_ref[...]) + bal_ref[...]) * (1.0 / GLA_TAU)
        p1, p2, p3 = _split3(la)
        bcs = _dot(tri, p1) + _dot(tri, p2) + _dot(tri, p3)
        b_last = bcs[CHUNK - 1:CHUNK, :]
        qb = (zq * jnp.exp(bcs)).astype(BF16)
        kb = (zk * jnp.exp(b_last - bcs)).astype(BF16)
        e_last = jnp.exp(b_last)

        for i in range(CHUNK // SUB):
            blk = slice(i * SUB, (i + 1) * SUB)
            q_blk = zq[blk, :]
            b_blk = bcs[blk, :]
            if i > 0:
                r_i = bcs[i * SUB - 1:i * SUB, :]
                q_i = (q_blk * jnp.exp(b_blk - r_i)).astype(BF16)
                kk = (zk * jnp.exp(jnp.where(row1 < i * SUB, r_i - bcs, NEG))).astype(BF16)

            att = [[jnp.zeros((SUBLANES, LANES), F32) for _ in range(SUB // SUBLANES)] for _ in range(H_GLA)]
            for sl in range(SUB):
                s = i * SUB + sl
                for g in range(sl // SUBLANES, SUB // SUBLANES):
                    rows_g = slice(g * SUBLANES, (g + 1) * SUBLANES)
                    diff = b_blk[rows_g, :] - bcs[s:s + 1, :]
                    if sl > g * SUBLANES:
                        diff = jnp.where(sub_row >= sl - g * SUBLANES, diff, NEG)
                    w = q_blk[rows_g, :] * zk[s:s + 1, :] * jnp.exp(diff)
                    for h in range(H_GLA):
                        col = jnp.sum(w[:, h * DKP:(h + 1) * DKP], axis=-1, keepdims=True)
                        att[h][g] = jnp.where(lane_id == s, col, att[h][g])
            for h in range(H_GLA):
                a_h = jnp.concatenate(att[h], axis=0)[:, :CHUNK]
                if i > 0:
                    a_h = a_h + _dot_nt(q_i[:, h * DKP:(h + 1) * DKP], kk[:, h * DKP:(h + 1) * DKP])
                att_s[h, blk, 0:CHUNK] = a_h

        for h in range(H_GLA):
            v_h = z_ref[rows, V0 + h * DVP:V0 + (h + 1) * DVP].astype(BF16)
            st_old = stg_ref[h]
            o_h = _dot(att_s[h, :, 0:CHUNK].astype(BF16), v_h)
            o_h = o_h + _dot_nt(qb[:, h * DKP:(h + 1) * DKP], st_old.astype(BF16))
            stg_ref[h] = e_last[:, h * DKP:(h + 1) * DKP] * st_old + _dot_tn(v_h, kb[:, h * DKP:(h + 1) * DKP])
            gate = z_ref[rows, GG0 + h * DVP:GG0 + (h + 1) * DVP]
            y = _head_norm(o_h, DV_GLA) * gng_ref[:, h * DVP:(h + 1) * DVP] * _silu(gate)
            mix_ref[rows, MG0 + h * DVP:MG0 + (h + 1) * DVP] = y.astype(BF16)

        cosv = cos_ref[rows, :]
        sinv = sin_ref[rows, :]
        for h in range(H_RET):
            lg = _ret_log_gamma(h)
            cols = slice(h * DK_RET, (h + 1) * DK_RET)
            xq = z_ref[rows, RQ0 + h * DK_RET:RQ0 + (h + 1) * DK_RET]
            xk = z_ref[rows, RK0 + h * DK_RET:RK0 + (h + 1) * DK_RET]
            q = (xq * cosv + pltpu.roll(xq, DK_RET // 2, axis=1) * sinv) * (DK_RET ** -0.5)
            k = xk * cosv + pltpu.roll(xk, DK_RET // 2, axis=1) * sinv
            v = z_ref[rows, RV0 + h * DK_RET:RV0 + (h + 1) * DK_RET].astype(BF16)
            decay = jnp.where(rel >= 0, jnp.exp(jnp.maximum(rel, 0.0) * lg), 0.0)
            q_dec = jnp.exp((rowf + 1.0) * lg)
            k_dec = jnp.exp((CHUNK - 1.0 - rowf) * lg)
            c_dec = math.exp(CHUNK * lg)
            qb16 = q.astype(BF16)
            att_r = _dot_nt(qb16, k.astype(BF16)) * decay
            s_old = str_ref[h]
            o = _dot(att_r.astype(BF16), v) + _dot(qb16, s_old.astype(BF16)) * q_dec
            str_ref[h] = c_dec * s_old + _dot_tn((k * k_dec).astype(BF16), v)
            gate = z_ref[rows, RG0 + h * DK_RET:RG0 + (h + 1) * DK_RET]
            y = _head_norm(o, DK_RET) * rng_ref[:, cols] * _silu(gate)
            mix_ref[rows, MR0 + h * DK_RET:MR0 + (h + 1) * DK_RET] = y.astype(BF16)

        return carry

    lax.fori_loop(0, tc // CHUNK, chunk_body, 0, unroll=2)

    cw = lcw_ref[...]
    xc = lcb_ref[...] + cw[0:1, :] * lxe_ref[pre - 3:pre - 3 + tc, :]
    xc = xc + cw[1:2, :] * lxe_ref[pre - 2:pre - 2 + tc, :]
    xc = xc + cw[2:3, :] * lxe_ref[pre - 1:pre - 1 + tc, :]
    xc = xc + cw[3:4, :] * lxe_ref[pre:pre + tc, :]
    a, u = _lru_gates(xc, wa_ref, ba_ref, wx_ref, bx_ref, lam_ref)
    a_ref[...] = a
    u_ref[...] = u

    def scan_body(t, h):
        h = a_ref[pl.ds(t, 1), :] * h + u_ref[pl.ds(t, 1), :]
        hs_ref[pl.ds(t, 1), :] = h
        return h

    h_fin = lax.fori_loop(0, tc, scan_body, h_ref[0:1, :], unroll=8)
    h_ref[0:1, :] = h_fin

    def out_body(c, carry):
        rows = pl.ds(pl.multiple_of(c * CHUNK, CHUNK), CHUNK)
        y = hs_ref[rows, :] * _gelu_tanh(z_ref[rows, LG0:LG0 + W_LRU])
        mix_ref[rows, ML0:ML0 + W_LRU] = y.astype(BF16)
        return carry

    lax.fori_loop(0, tc // CHUNK, out_body, 0)

    lxe_ref[0:pre, :] = lxe_ref[tc:tc + pre, :]

    @pl.when(t_id == nt - 1)
    def _():
        for h in range(H_GLA):
            sg_ref[h] = stg_ref[h].T[:DK_GLA, :DV_GLA]
        sr_ref[...] = str_ref[...]
        hl_ref[...] = h_fin
        lc_ref[...] = lxe_ref[tc + pre - (LRU_K - 1):tc + pre, :]


def _mixer_prompt_call(z, cos_t, sin_t, pw, *, layer):
    tc = MIX_TC
    nt = TP // tc

    def lspec(shape):
        return pl.BlockSpec((None,) + shape, lambda b, t: (layer,) + (0,) * len(shape))

    in_specs = [
        pl.BlockSpec((tc, NZ), lambda b, t: (b * nt + t, 0)),
        pl.BlockSpec((tc, LANES), lambda b, t: (t, 0)),
        pl.BlockSpec((tc, LANES), lambda b, t: (t, 0)),
        lspec((LANES, H_GLA * DKP)), lspec((1, H_GLA * DKP)), lspec((1, H_GLA * DVP)),
        lspec((LRU_K, W_LRU)), lspec((1, W_LRU)),
        lspec((W_LRU, W_LRU)), lspec((1, W_LRU)), lspec((W_LRU, W_LRU)), lspec((1, W_LRU)), lspec((1, W_LRU)),
        lspec((1, H_RET * DK_RET)),
    ]
    out_specs = [
        pl.BlockSpec((tc, NMIX), lambda b, t: (b * nt + t, 0)),
        pl.BlockSpec((None, H_GLA, DK_GLA, DV_GLA), lambda b, t: (b, 0, 0, 0)),
        pl.BlockSpec((None, H_RET, DK_RET, DK_RET), lambda b, t: (b, 0, 0, 0)),
        pl.BlockSpec((None, 1, W_LRU), lambda b, t: (b, 0, 0)),
        pl.BlockSpec((None, LRU_K - 1, W_LRU), lambda b, t: (b, 0, 0)),
    ]
    out_shape = [
        jax.ShapeDtypeStruct((BP * TP, NMIX), BF16),
        jax.ShapeDtypeStruct((BP, H_GLA, DK_GLA, DV_GLA), F32),
        jax.ShapeDtypeStruct((BP, H_RET, DK_RET, DK_RET), F32),
        jax.ShapeDtypeStruct((BP, 1, W_LRU), F32),
        jax.ShapeDtypeStruct((BP, LRU_K - 1, W_LRU), F32),
    ]
    scratch = [
        pltpu.VMEM((H_GLA, DVP, DKP), F32),
        pltpu.VMEM((H_RET, DK_RET, DK_RET), F32),
        pltpu.VMEM((SUBLANES, W_LRU), F32),
        pltpu.VMEM((SUBLANES + tc, W_LRU), F32),
        pltpu.VMEM((tc, W_LRU), F32), pltpu.VMEM((tc, W_LRU), F32), pltpu.VMEM((tc, W_LRU), F32),
        pltpu.VMEM((H_GLA, CHUNK, LANES), F32),
    ]
    return pl.pallas_call(
        _mixer_prompt_kernel,
        grid=(BP, nt),
        in_specs=in_specs,
        out_specs=out_specs,
        out_shape=out_shape,
        scratch_shapes=scratch,
        compiler_params=_cparams(("arbitrary", "arbitrary")),
        name="mixer_p",
    )(z, cos_t, sin_t, pw["w_alpha"], pw["b_alpha"], pw["gla_norm_g"], pw["lru_conv_w"], pw["lru_conv_b"],
      pw["lru_wa"], pw["lru_ba"], pw["lru_wx"], pw["lru_bx"], pw["lru_lam"], pw["ret_norm_g"])


MIX_BB = 8


def _mixer_sample_kernel(z_ref, cos_ref, sin_ref, wal_ref, bal_ref, gng_ref, lcw_ref, lcb_ref,
                         wa_ref, ba_ref, wx_ref, bx_ref, lam_ref, rng_ref,
                         sg0_ref, sr0_ref, h0_ref, lc0_ref,
                         mix_ref, sg_ref, sr_ref, hl_ref, lc_ref,
                         qb_s, kb_s, v_s, og_s, rq_s, rk_s, rv_s, or_s, sp_ref, et_ref):
    bb = MIX_BB
    nrow = TS * bb

    @pl.when(pl.program_id(0) == 0)
    def _():
        sp_ref[...] = jnp.zeros_like(sp_ref)
        et_ref[...] = jnp.zeros_like(et_ref)

    def zcols(c0, width):
        return z_ref[:, :, c0:c0 + width].reshape(nrow, width)

    def slab(x, t):
        return x[t * bb:(t + 1) * bb, :]

    zq = zcols(Q0, H_GLA * DKP) * (DK_GLA ** -0.5)
    zk = zcols(K0, H_GLA * DKP)
    zv = zcols(V0, H_GLA * DVP)
    la = _log_sigmoid(_dot(zcols(LR0, LANES).astype(BF16), wal_ref[...]) + bal_ref[...]) * (1.0 / GLA_TAU)
    bs = [slab(la, 0)]
    for t in range(1, TS):
        bs.append(bs[-1] + slab(la, t))
    b_last = bs[-1]
    e_last = jnp.exp(b_last)
    o_t = []
    for t in range(TS):
        acc = None
        for s in range(t + 1):
            w = slab(zq, t) * slab(zk, s) * jnp.exp(bs[t] - bs[s])
            parts = []
            for h in range(H_GLA):
                a_ts = jnp.sum(w[:, h * DKP:(h + 1) * DKP], axis=-1, keepdims=True)
                parts.append(a_ts * slab(zv, s)[:, h * DVP:(h + 1) * DVP])
            contrib = jnp.concatenate(parts, axis=1)
            acc = contrib if acc is None else acc + contrib
        o_t.append(acc)
    o_intra = jnp.concatenate(o_t, axis=0)
    b_all = jnp.concatenate(bs, axis=0)
    qb = zq * jnp.exp(b_all)
    kb = zk * jnp.exp(jnp.concatenate([b_last] * TS, axis=0) - b_all)
    for h in range(H_GLA):
        qb_s[h] = qb[:, h * DKP:(h + 1) * DKP]
        kb_s[h] = kb[:, h * DKP:(h + 1) * DKP]
        for p in range(DVP // LANES):
            v_s[h * (DVP // LANES) + p] = zv[:, h * DVP + p * LANES:h * DVP + (p + 1) * LANES]
        et_ref[h, 0:bb, :] = e_last[:, h * DKP:(h + 1) * DKP]
    e_cols = [et_ref[h].T for h in range(H_GLA)]

    for b in range(bb):
        seq = pl.ds(b, TS, stride=bb)
        for h in range(H_GLA):
            s0 = sg0_ref[b, h]
            sp_ref[0:DK_GLA, 0:DV_GLA] = s0
            q_bh = qb_s[h, seq, :]
            k_bh = kb_s[h, seq, :]
            v_bh = jnp.concatenate([v_s[h * (DVP // LANES) + p, seq, :] for p in range(DVP // LANES)], axis=1)
            o_bh = _dot(q_bh, sp_ref[...])
            for p in range(DVP // LANES):
                og_s[h * (DVP // LANES) + p, seq, :] = o_bh[:, p * LANES:(p + 1) * LANES]
            upd = _dot_tn(k_bh, v_bh)
            s_new = e_cols[h][:, b:b + 1] * sp_ref[...] + upd
            sg_ref[b, h] = s_new[:DK_GLA, :DV_GLA]

    og = jnp.concatenate([og_s[i] for i in range(H_GLA * DVP // LANES)], axis=1) + o_intra
    for h in range(H_GLA):
        cols = slice(h * DVP, (h + 1) * DVP)
        y = _head_norm(og[:, cols], DV_GLA) * gng_ref[:, cols] * _silu(zcols(GG0 + h * DVP, DVP))
        mix_ref[:, :, MG0 + h * DVP:MG0 + (h + 1) * DVP] = y.reshape(TS, bb, DVP)

    cosv = jnp.concatenate([jnp.broadcast_to(cos_ref[t:t + 1, :], (bb, LANES)) for t in range(TS)], axis=0)
    sinv = jnp.concatenate([jnp.broadcast_to(sin_ref[t:t + 1, :], (bb, LANES)) for t in range(TS)], axis=0)
    trow = lax.broadcasted_iota(jnp.int32, (TS, LANES), 0).astype(F32)
    r_intra = []
    for h in range(H_RET):
        lg = _ret_log_gamma(h)
        xq = zcols(RQ0 + h * DK_RET, DK_RET)
        xk = zcols(RK0 + h * DK_RET, DK_RET)
        q = (xq * cosv + pltpu.roll(xq, DK_RET // 2, axis=1) * sinv) * (DK_RET ** -0.5)
        k = xk * cosv + pltpu.roll(xk, DK_RET // 2, axis=1) * sinv
        v = zcols(RV0 + h * DK_RET, DK_RET)
        rq_s[h] = q
        rk_s[h] = k
        rv_s[h] = v
        outs = []
        for t in range(TS):
            acc = None
            for s in range(t + 1):
                a_ts = jnp.sum(slab(q, t) * slab(k, s), axis=-1, keepdims=True) * math.exp((t - s) * lg)
                contrib = a_ts * slab(v, s)
                acc = contrib if acc is None else acc + contrib
            outs.append(acc)
        r_intra.append(jnp.concatenate(outs, axis=0))

    for b in range(bb):
        seq = pl.ds(b, TS, stride=bb)
        for h in range(H_RET):
            lg = _ret_log_gamma(h)
            q_dec = jnp.exp((trow + 1.0) * lg)
            k_dec = jnp.exp((TS - 1.0 - trow) * lg)
            s0 = sr0_ref[b, h]
            q_bh = rq_s[h, seq, :]
            k_bh = rk_s[h, seq, :]
            v_bh = rv_s[h, seq, :]
            or_s[h, seq, :] = _dot(q_bh, s0) * q_dec
            sr_ref[b, h] = math.exp(TS * lg) * s0 + _dot_tn(k_bh * k_dec, v_bh)

    for h in range(H_RET):
        cols = slice(h * DK_RET, (h + 1) * DK_RET)
        o = or_s[h] + r_intra[h]
        y = _head_norm(o, DK_RET) * rng_ref[:, cols] * _silu(zcols(RG0 + h * DK_RET, DK_RET))
        mix_ref[:, :, MR0 + h * DK_RET:MR0 + (h + 1) * DK_RET] = y.reshape(TS, bb, DK_RET)

    lx = zcols(LX0, W_LRU)
    xe = [lc0_ref[:, k * W_LRU:(k + 1) * W_LRU] for k in range(LRU_K - 1)] + [slab(lx, t) for t in range(TS)]
    cw = lcw_ref[...]
    xc = []
    for t in range(TS):
        acc = lcb_ref[...] + cw[0:1, :] * xe[t]
        for k in range(1, LRU_K):
            acc = acc + cw[k:k + 1, :] * xe[t + k]
        xc.append(acc)
    a, u = _lru_gates(jnp.concatenate(xc, axis=0), wa_ref, ba_ref, wx_ref, bx_ref, lam_ref)
    h = h0_ref[...]
    hs = []
    for t in range(TS):
        h = slab(a, t) * h + slab(u, t)
        hs.append(h)
    y = jnp.concatenate(hs, axis=0) * _gelu_tanh(zcols(LG0, W_LRU))
    mix_ref[:, :, ML0:ML0 + W_LRU] = y.reshape(TS, bb, W_LRU)
    hl_ref[...] = h
    for k in range(LRU_K - 1):
        lc_ref[:, k * W_LRU:(k + 1) * W_LRU] = xe[TS + k]


def _mixer_sample_call(z, cos_t, sin_t, pw, st_gla, st_ret, st_lru, st_lconv, *, layer):
    bb = MIX_BB

    def lspec(shape):
        return pl.BlockSpec((None,) + shape, lambda j: (layer,) + (0,) * len(shape))

    in_specs = [
        pl.BlockSpec((TS, bb, NZ), lambda j: (0, j, 0)),
        pl.BlockSpec((TS, LANES), lambda j: (0, 0)),
        pl.BlockSpec((TS, LANES), lambda j: (0, 0)),
        lspec((LANES, H_GLA * DKP)), lspec((1, H_GLA * DKP)), lspec((1, H_GLA * DVP)),
        lspec((LRU_K, W_LRU)), lspec((1, W_LRU)),
        lspec((W_LRU, W_LRU)), lspec((1, W_LRU)), lspec((W_LRU, W_LRU)), lspec((1, W_LRU)), lspec((1, W_LRU)),
        lspec((1, H_RET * DK_RET)),
        pl.BlockSpec((None, bb, H_GLA, DK_GLA, DV_GLA), lambda j: (layer, j, 0, 0, 0)),
        pl.BlockSpec((None, bb, H_RET, DK_RET, DK_RET), lambda j: (layer, j, 0, 0, 0)),
        pl.BlockSpec((None, bb, W_LRU), lambda j: (layer, j, 0)),
        pl.BlockSpec((None, bb, (LRU_K - 1) * W_LRU), lambda j: (layer, j, 0)),
    ]
    out_specs = [
        pl.BlockSpec((TS, bb, NMIX), lambda j: (0, j, 0)),
        pl.BlockSpec((bb, H_GLA, DK_GLA, DV_GLA), lambda j: (j, 0, 0, 0)),
        pl.BlockSpec((bb, H_RET, DK_RET, DK_RET), lambda j: (j, 0, 0, 0)),
        pl.BlockSpec((bb, W_LRU), lambda j: (j, 0)),
        pl.BlockSpec((bb, (LRU_K - 1) * W_LRU), lambda j: (j, 0)),
    ]
    out_shape = [
        jax.ShapeDtypeStruct((TS, BS, NMIX), F32),
        jax.ShapeDtypeStruct((BS, H_GLA, DK_GLA, DV_GLA), F32),
        jax.ShapeDtypeStruct((BS, H_RET, DK_RET, DK_RET), F32),
        jax.ShapeDtypeStruct((BS, W_LRU), F32),
        jax.ShapeDtypeStruct((BS, (LRU_K - 1) * W_LRU), F32),
    ]
    nrow = TS * bb
    scratch = [
        pltpu.VMEM((H_GLA, nrow, LANES), F32), pltpu.VMEM((H_GLA, nrow, LANES), F32),
        pltpu.VMEM((H_GLA * DVP // LANES, nrow, LANES), F32), pltpu.VMEM((H_GLA * DVP // LANES, nrow, LANES), F32),
        pltpu.VMEM((H_RET, nrow, LANES), F32), pltpu.VMEM((H_RET, nrow, LANES), F32),
        pltpu.VMEM((H_RET, nrow, LANES), F32), pltpu.VMEM((H_RET, nrow, LANES), F32),
        pltpu.VMEM((DKP, DVP), F32),
        pltpu.VMEM((H_GLA, LANES, LANES), F32),
    ]
    return pl.pallas_call(
        _mixer_sample_kernel,
        grid=(BS // bb,),
        in_specs=in_specs,
        out_specs=out_specs,
        out_shape=out_shape,
        scratch_shapes=scratch,
        compiler_params=_cparams(("arbitrary",)),
        name="mixer_s",
    )(z, cos_t, sin_t, pw["w_alpha"], pw["b_alpha"], pw["gla_norm_g"], pw["lru_conv_w"], pw["lru_conv_b"],
      pw["lru_wa"], pw["lru_ba"], pw["lru_wx"], pw["lru_bx"], pw["lru_lam"], pw["ret_norm_g"],
      st_gla, st_ret, st_lru, st_lconv)


def _pad_heads(w, heads, d, dp):
    lead = w.shape[:-1]
    w = w.reshape(lead + (heads, d))
    w = jnp.pad(w, [(0, 0)] * len(lead) + [(0, 0), (0, dp - d)])
    return w.reshape(lead + (heads * dp,))


def _split_cols(w, sizes):
    out, off = [], 0
    for s in sizes:
        out.append(w[..., off:off + s])
        off += s
    return out


def _rope_tables(start, length):
    half = DK_RET // 2
    freqs = ROPE_BASE ** (-jnp.arange(half, dtype=F32) / half)
    pos = start + jnp.arange(length, dtype=jnp.int32)
    ang = pos.astype(F32)[:, None] * freqs[None, :]
    cos, sin = jnp.cos(ang), jnp.sin(ang)
    return jnp.concatenate([cos, cos], axis=1), jnp.concatenate([-sin, sin], axis=1)


def _prep_weights(w_in, gla_w_alpha, gla_b_alpha, gla_norm_g, lru_conv_w, lru_conv_b, lru_w_a, lru_b_a,
                  lru_w_x, lru_b_x, lru_lambda, ret_norm_g, w_out, ffn_w_up, ffn_conv_w, ffn_conv_b, ffn_w_down):
    w_gla, w_ret = H_GLA * DV_GLA, H_RET * DK_RET
    sizes = (H_GLA * DK_GLA, H_GLA * DK_GLA, w_gla, GLA_RANK, w_gla, W_LRU, W_LRU, w_ret, w_ret, w_ret, w_ret)
    gq, gk, gv, glr, gg, lx, lg, rq, rk, rv, rg = _split_cols(w_in, sizes)
    w_in_p = jnp.concatenate([
        _pad_heads(gq, H_GLA, DK_GLA, DKP), _pad_heads(gk, H_GLA, DK_GLA, DKP), _pad_heads(gv, H_GLA, DV_GLA, DVP),
        jnp.pad(glr, ((0, 0), (0, 0), (0, LANES - GLA_RANK))), _pad_heads(gg, H_GLA, DV_GLA, DVP),
        lx, lg, rq, rk, rv, rg, jnp.zeros((DEPTH, D, NZ - NZ_USED), F32)], axis=-1).astype(BF16)
    w_alpha = jnp.pad(_pad_heads(gla_w_alpha, H_GLA, DK_GLA, DKP), ((0, 0), (0, LANES - GLA_RANK), (0, 0))).astype(BF16)
    eye = jnp.eye(H_LRU, dtype=F32)

    def block_diag(w):
        return jnp.einsum("lhij,hg->lhigj", w, eye).reshape(DEPTH, W_LRU, W_LRU).astype(BF16)

    wo_g, wo_l, wo_r = _split_cols(jnp.swapaxes(w_out, 1, 2), (w_gla, W_LRU, w_ret))
    w_out_p = jnp.swapaxes(jnp.concatenate([_pad_heads(wo_g, H_GLA, DV_GLA, DVP), wo_l, wo_r], axis=-1), 1, 2)
    return dict(
        w_in=w_in_p,
        w_alpha=w_alpha,
        b_alpha=_pad_heads(gla_b_alpha, H_GLA, DK_GLA, DKP)[:, None, :],
        gla_norm_g=_pad_heads(gla_norm_g, H_GLA, DV_GLA, DVP)[:, None, :],
        lru_conv_w=lru_conv_w,
        lru_conv_b=lru_conv_b[:, None, :],
        lru_wa=block_diag(lru_w_a), lru_ba=lru_b_a[:, None, :],
        lru_wx=block_diag(lru_w_x), lru_bx=lru_b_x[:, None, :],
        lru_lam=lru_lambda[:, None, :],
        ret_norm_g=ret_norm_g[:, None, :],
        w_out=w_out_p.astype(BF16),
        w_up=ffn_w_up.astype(BF16),
        ffn_cw=ffn_conv_w,
        ffn_cb=ffn_conv_b[:, None, :],
        w_down=ffn_w_down.astype(BF16),
    )


def _run_group(x, mod, pw, norm1_g, norm2_g, tables, states, *, prompt):
    tm = 1024
    tn_in, tn_out, tn_down = 1024, 1024, 512
    outs = []
    for l in range(DEPTH):
        z = _proj_call(x, norm1_g, mod, pw["w_in"], layer=l, sc_chunk=1, sh_chunk=0, prompt=prompt, tm=tm, tn=tn_in)
        if prompt:
            mixed, s_gla, s_ret, s_lru, s_lconv = _mixer_prompt_call(z, *tables, pw, layer=l)
            s_lru = s_lru.reshape(BP, W_LRU)
        else:
            st_gla, st_ret, st_lru, st_lconv, _ = states
            mixed, s_gla, s_ret, s_lru, s_lconv = _mixer_sample_call(
                z.reshape(TS, BS, NZ), *tables, pw, st_gla, st_ret, st_lru, st_lconv, layer=l)
            mixed = mixed.reshape(TS * BS, NMIX)
            s_lconv = s_lconv.reshape(BS, LRU_K - 1, W_LRU)
        x = _resid_call(mixed, pw["w_out"], x, mod, layer=l, g_chunk=2, prompt=prompt, tm=tm, tn=tn_out,
                        name="out_proj_p" if prompt else "out_proj_s")
        act, s_fg, s_fv = _ffn_up_call(x, norm2_g, mod, pw["w_up"], pw["ffn_cw"], pw["ffn_cb"],
                                       None if prompt else states[4], layer=l, prompt=prompt, tm=tm)
        x = _resid_call(act, pw["w_down"], x, mod, layer=l, g_chunk=5, prompt=prompt, tm=tm, tn=tn_down,
                        name="ffn_down_p" if prompt else "ffn_down_s")
        s_fconv = jnp.concatenate([s_fg, s_fv], axis=-1)
        if prompt:
            s_fconv = s_fconv[TP // tm - 1::TP // tm]
        else:
            s_fconv = jnp.swapaxes(s_fconv, 0, 1)
        outs.append((s_gla, s_ret, s_lru, s_lconv, s_fconv))
    return x, [jnp.stack(o) for o in zip(*outs)]


def kernel(x_prompt, x_sample, state_gla, state_ret, state_lru, state_lru_conv, state_ffn_conv, c_prompt, c_sample,
           norm1_g, norm2_g, final_g, w_ada, b_ada, w_in, gla_w_alpha, gla_b_alpha, gla_norm_g, lru_conv_w,
           lru_conv_b, lru_w_a, lru_b_a, lru_w_x, lru_b_x, lru_lambda, ret_norm_g, w_out, ffn_w_up, ffn_conv_w,
           ffn_conv_b, ffn_w_down):
    pw = _prep_weights(w_in, gla_w_alpha, gla_b_alpha, gla_norm_g, lru_conv_w, lru_conv_b, lru_w_a, lru_b_a,
                       lru_w_x, lru_b_x, lru_lambda, ret_norm_g, w_out, ffn_w_up, ffn_conv_w, ffn_conv_b, ffn_w_down)
    cp8 = jnp.pad(c_prompt, ((0, SUBLANES - BP), (0, 0)))
    mod_p, mod_s = _mod_call(cp8, c_sample, w_ada, b_ada[:, None, :])
    n1 = norm1_g[:, None, :]
    n2 = norm2_g[:, None, :]

    xp = x_prompt.reshape(BP * TP, D)
    xs = jnp.swapaxes(x_sample, 0, 1).reshape(TS * BS, D)
    st_s = (state_gla, state_ret, state_lru, state_lru_conv.reshape(DEPTH, BS, (LRU_K - 1) * W_LRU),
            jnp.swapaxes(state_ffn_conv, 1, 2))

    xp, outs_p = _run_group(xp, mod_p, pw, n1, n2, _rope_tables(0, TP), None, prompt=True)
    xs, outs_s = _run_group(xs, mod_s, pw, n1, n2, _rope_tables(PAST, TS), st_s, prompt=False)

    fg = final_g[None, :]
    y_p = _final_call(xp, fg).reshape(BP, TP, D)
    y_s = jnp.swapaxes(_final_call(xs, fg).reshape(TS, BS, D), 0, 1)
    return (y_p, y_s, *outs_p, *outs_s)
```

```python
import functools
import math

import jax
import jax.numpy as jnp
from jax import lax
from jax.experimental import pallas as pl
from jax.experimental.pallas import tpu as pltpu

F32 = jnp.float32
BF16 = jnp.bfloat16

D = 2048
DEPTH = 4
BP, TP = 4, 2048
BS, TS = 128, 8
PAST = 16384
H_GLA, DK_GLA, DV_GLA = 4, 96, 192
GLA_RANK = 16
GLA_TAU = 16.0
W_LRU, H_LRU, BLK_LRU = 768, 8, 96
LRU_K = 4
LRU_C = 8.0
H_RET, DK_RET = 4, 128
ROPE_BASE = 10000.0
D_FF = 5632
FFN_K = 3
CHUNK = 64
EPS = 1e-6

LANES = 128
SUBLANES = 8
DKP, DVP = 128, 256

Q0 = 0
K0 = Q0 + H_GLA * DKP
V0 = K0 + H_GLA * DKP
LR0 = V0 + H_GLA * DVP
GG0 = LR0 + LANES
LX0 = GG0 + H_GLA * DVP
LG0 = LX0 + W_LRU
RQ0 = LG0 + W_LRU
RK0 = RQ0 + H_RET * DK_RET
RV0 = RK0 + H_RET * DK_RET
RG0 = RV0 + H_RET * DK_RET
NZ_USED = RG0 + H_RET * DK_RET
NZ = -(-NZ_USED // 1024) * 1024
MG0 = 0
ML0 = H_GLA * DVP
MR0 = ML0 + W_LRU
NMIX = MR0 + H_RET * DK_RET

SUB = 16
NEG = -1e30
LOG2E = math.log2(math.e)
FF_TN = 512
FF_NJ = D_FF // FF_TN

VMEM_LIMIT = 56 * 1024 * 1024


def _cparams(sem, flags=None):
    return pltpu.CompilerParams(dimension_semantics=sem, vmem_limit_bytes=VMEM_LIMIT, flags=flags)


def _sigmoid(x):
    return 1.0 / (1.0 + jnp.exp(-x))


def _silu(x):
    return x * _sigmoid(x)


def _gelu_tanh(x):
    c = math.sqrt(2.0 / math.pi)
    return 0.5 * x * (1.0 + jnp.tanh(c * (x + 0.044715 * (x * x * x))))


def _log_sigmoid(x):
    return jnp.minimum(x, 0.0) - jnp.log1p(jnp.exp(-jnp.abs(x)))


def _softplus(x):
    return jnp.maximum(x, 0.0) + jnp.log1p(jnp.exp(-jnp.abs(x)))


def _dot(a, b):
    return jnp.dot(a, b, preferred_element_type=F32)


def _dot_nt(a, b):
    return lax.dot_general(a, b, (((1,), (1,)), ((), ())), preferred_element_type=F32)


def _dot_tn(a, b):
    return lax.dot_general(a, b, (((0,), (0,)), ((), ())), preferred_element_type=F32)


def _norm_mod(x, g, sc, sh):
    ms = jnp.mean(x * x, axis=-1, keepdims=True)
    return (x * lax.rsqrt(ms + EPS)) * g * (1.0 + sc) + sh


def _head_norm(o, width):
    ms = jnp.sum(o * o, axis=-1, keepdims=True) * (1.0 / width)
    return o * lax.rsqrt(ms + EPS)


def _mod_kernel(cp_ref, cs_ref, w_ref, b_ref, op_ref, os_ref):
    w = w_ref[...].astype(BF16)
    b = b_ref[...]
    op_ref[...] = _dot(_silu(cp_ref[...]).astype(BF16), w) + b
    os_ref[...] = _dot(_silu(cs_ref[...]).astype(BF16), w) + b


def _mod_call(cp8, cs, w_ada, b_ada):
    tn = 1024
    n = 6 * D
    return pl.pallas_call(
        _mod_kernel,
        grid=(DEPTH, n // tn),
        in_specs=[
            pl.BlockSpec((SUBLANES, D), lambda l, j: (0, 0)),
            pl.BlockSpec((BS, D), lambda l, j: (0, 0)),
            pl.BlockSpec((None, D, tn), lambda l, j: (l, 0, j)),
            pl.BlockSpec((None, 1, tn), lambda l, j: (l, 0, j)),
        ],
        out_specs=[
            pl.BlockSpec((None, SUBLANES, tn), lambda l, j: (l, 0, j)),
            pl.BlockSpec((None, BS, tn), lambda l, j: (l, 0, j)),
        ],
        out_shape=[
            jax.ShapeDtypeStruct((DEPTH, SUBLANES, n), F32),
            jax.ShapeDtypeStruct((DEPTH, BS, n), F32),
        ],
        compiler_params=_cparams(("arbitrary", "arbitrary")),
        name="adaln_mod",
    )(cp8, cs, w_ada, b_ada)


ROW_CHUNK = 128


def _fill_hn(x_ref, g_ref, sc_ref, sh_ref, hn_ref, *, tile, tm, prompt, seq_tiles):
    g = g_ref[...]
    if prompt:
        b = tile // seq_tiles
        sc = sc_ref[pl.ds(b, 1), :]
        sh = sh_ref[pl.ds(b, 1), :]

    def body(r, carry):
        rows = pl.ds(pl.multiple_of(r * ROW_CHUNK, ROW_CHUNK), ROW_CHUNK)
        if prompt:
            y = _norm_mod(x_ref[rows, :], g, sc, sh)
        else:
            y = _norm_mod(x_ref[rows, :], g, sc_ref[...], sh_ref[...])
        hn_ref[rows, :] = y.astype(BF16)
        return carry

    lax.fori_loop(0, tm // ROW_CHUNK, body, 0)


def _mod_specs(layer, chunks, prompt, width, col_of):
    rows = SUBLANES if prompt else BS
    per = D // width
    return [pl.BlockSpec((None, rows, width), (lambda i, j, c=c: (layer, 0, c * per + col_of(i, j)))) for c in chunks]


def _proj_kernel(x_ref, g_ref, sc_ref, sh_ref, w_ref, o_ref, hn_ref, *, tm, prompt, seq_tiles):
    @pl.when(pl.program_id(1) == 0)
    def _():
        _fill_hn(x_ref, g_ref, sc_ref, sh_ref, hn_ref, tile=pl.program_id(0), tm=tm, prompt=prompt,
                 seq_tiles=seq_tiles)

    o_ref[...] = _dot_nt(hn_ref[...], w_ref[...])


def _proj_call(x, norm_g, mod, w, *, layer, sc_chunk, sh_chunk, prompt, tm, tn):
    m = x.shape[0]
    n = w.shape[-2]
    kern = functools.partial(_proj_kernel, tm=tm, prompt=prompt, seq_tiles=TP // tm if prompt else 1)
    sc_spec, sh_spec = _mod_specs(layer, (sc_chunk, sh_chunk), prompt, D, lambda i, j: 0)
    return pl.pallas_call(
        kern,
        grid=(m // tm, n // tn),
        in_specs=[
            pl.BlockSpec((tm, D), lambda i, j: (i, 0)),
            pl.BlockSpec((None, 1, D), lambda i, j: (layer, 0, 0)),
            sc_spec,
            sh_spec,
            pl.BlockSpec((None, tn, D), lambda i, j: (layer, j, 0)),
        ],
        out_specs=pl.BlockSpec((tm, tn), lambda i, j: (i, j)),
        out_shape=jax.ShapeDtypeStruct((m, n), F32),
        scratch_shapes=[pltpu.VMEM((tm, D), BF16)],
        compiler_params=_cparams(("arbitrary", "arbitrary")),
        name="in_proj_p" if prompt else "in_proj_s",
    )(x, norm_g, mod, mod, w)


def _resid_kernel(a_ref, w_ref, x_ref, g_ref, o_ref, *, tm, prompt, seq_tiles):
    acc = _dot(a_ref[...].astype(BF16), w_ref[...])
    if prompt:
        b = pl.program_id(0) // seq_tiles
        o_ref[...] = x_ref[...] + g_ref[pl.ds(b, 1), :] * acc
    else:
        g = g_ref[...]
        for t in range(tm // BS):
            rows = slice(t * BS, (t + 1) * BS)
            o_ref[rows, :] = x_ref[rows, :] + g * acc[rows, :]


def _resid_call(a, w, x, mod, *, layer, g_chunk, prompt, tm, tn, name):
    m, k = a.shape
    kern = functools.partial(_resid_kernel, tm=tm, prompt=prompt, seq_tiles=TP // tm if prompt else 1)
    (g_spec,) = _mod_specs(layer, (g_chunk,), prompt, tn, lambda i, j: j)
    return pl.pallas_call(
        kern,
        grid=(m // tm, D // tn),
        in_specs=[
            pl.BlockSpec((tm, k), lambda i, j: (i, 0)),
            pl.BlockSpec((None, k, tn), lambda i, j: (layer, 0, j)),
            pl.BlockSpec((tm, tn), lambda i, j: (i, j)),
            g_spec,
        ],
        out_specs=pl.BlockSpec((tm, tn), lambda i, j: (i, j)),
        out_shape=jax.ShapeDtypeStruct((m, D), F32),
        compiler_params=_cparams(("arbitrary", "arbitrary")),
        name=name,
    )(a, w, x, mod)


def _conv3_rows(ue_ref, cols, cw, cb, *, tm, pre):
    first_row = lax.broadcasted_iota(jnp.int32, (SUBLANES, 1), 0) == 0
    w0, w1, w2 = cw[0:1, :], cw[1:2, :], cw[2:3, :]
    x = ue_ref[pre - SUBLANES:pre, cols]
    rot_a = pltpu.roll(w0 * x, 1, axis=0)
    rot_t = pltpu.roll(w1 * x + rot_a, 1, axis=0)
    for g in range(tm // SUBLANES):
        x = ue_ref[pre + g * SUBLANES:pre + (g + 1) * SUBLANES, cols]
        rot_a_new = pltpu.roll(w0 * x, 1, axis=0)
        t = w1 * x + jnp.where(first_row, rot_a, rot_a_new)
        rot_t_new = pltpu.roll(t, 1, axis=0)
        yield g, (cb + w2 * x) + jnp.where(first_row, rot_t, rot_t_new)
        rot_a, rot_t = rot_a_new, rot_t_new


def _ffn_up_kernel(*refs, tm, prompt, seq_tiles, pre, shift):
    (x_ref, g_ref, sc_ref, sh_ref, wg_ref, wv_ref, cwg_ref, cwv_ref, cbg_ref, cbv_ref) = refs[:10]
    if prompt:
        act_ref, stg_ref, stv_ref, hn_ref, ue_ref, carry_ref = refs[10:]
    else:
        s0g_ref, s1g_ref, s0v_ref, s1v_ref, act_ref, stg_ref, stv_ref, hn_ref, ue_ref = refs[10:]
    i = pl.program_id(0)
    j = pl.program_id(1)
    gate_cols = slice(0, FF_TN)
    val_cols = slice(FF_TN, 2 * FF_TN)

    @pl.when(j == 0)
    def _():
        _fill_hn(x_ref, g_ref, sc_ref, sh_ref, hn_ref, tile=i, tm=tm, prompt=prompt, seq_tiles=seq_tiles)

    if prompt:
        first = (i % seq_tiles) == 0

        @pl.when(first)
        def _():
            ue_ref[0:pre, :] = jnp.zeros((pre, 2 * FF_TN), F32)

        @pl.when(jnp.logical_not(first))
        def _():
            ue_ref[0:pre, :] = carry_ref[j]
    else:
        ue_ref[0:BS, gate_cols] = s0g_ref[...]
        ue_ref[0:BS, val_cols] = s0v_ref[...]
        ue_ref[BS:2 * BS, gate_cols] = s1g_ref[...]
        ue_ref[BS:2 * BS, val_cols] = s1v_ref[...]

    hn = hn_ref[...]
    ue_ref[pre:pre + tm, gate_cols] = _dot(hn, wg_ref[...])
    ue_ref[pre:pre + tm, val_cols] = _dot(hn, wv_ref[...])

    cwg, cwv, cbg, cbv = cwg_ref[...], cwv_ref[...], cbg_ref[...], cbv_ref[...]
    if prompt:
        gates = _conv3_rows(ue_ref, gate_cols, cwg, cbg, tm=tm, pre=pre)
        vals = _conv3_rows(ue_ref, val_cols, cwv, cbv, tm=tm, pre=pre)
        pending = []
        for (g, gate), (_, val) in zip(gates, vals):
            pending.append(_silu(gate) * val)
            if len(pending) == 2:
                act_ref[(g - 1) * SUBLANES:(g + 1) * SUBLANES, :] = jnp.concatenate(pending, axis=0).astype(BF16)
                pending = []
        carry_ref[j] = ue_ref[tm:tm + pre, :]
        stg_ref[...] = ue_ref[pre + tm - (FFN_K - 1):pre + tm, gate_cols]
        stv_ref[...] = ue_ref[pre + tm - (FFN_K - 1):pre + tm, val_cols]
    else:
        def conv(base, cols, cw, cb):
            u = cb + cw[0:1, :] * ue_ref[base + pre - 2 * shift:base + pre - 2 * shift + rc, cols]
            u = u + cw[1:2, :] * ue_ref[base + pre - shift:base + pre - shift + rc, cols]
            return u + cw[2:3, :] * ue_ref[base + pre:base + pre + rc, cols]

        rc = 16
        for r in range(tm // rc):
            gate = conv(r * rc, gate_cols, cwg, cbg)
            val = conv(r * rc, val_cols, cwv, cbv)
            act_ref[r * rc:(r + 1) * rc, :] = (_silu(gate) * val).astype(BF16)
        for k in range(FFN_K - 1):
            rows = slice(pre + tm - (FFN_K - 1 - k) * BS, pre + tm - (FFN_K - 2 - k) * BS)
            stg_ref[k] = ue_ref[rows, gate_cols]
            stv_ref[k] = ue_ref[rows, val_cols]


def _ffn_up_call(x, norm_g, mod, w, cw, cb, st_in, *, layer, prompt, tm):
    m = x.shape[0]
    seq_tiles = TP // tm if prompt else 1
    pre = SUBLANES if prompt else (FFN_K - 1) * BS
    shift = 1 if prompt else BS
    kern = functools.partial(_ffn_up_kernel, tm=tm, prompt=prompt, seq_tiles=seq_tiles, pre=pre, shift=shift)
    mod_rows = SUBLANES if prompt else BS
    in_specs = [
        pl.BlockSpec((tm, D), lambda i, j: (i, 0)),
        pl.BlockSpec((None, 1, D), lambda i, j: (layer, 0, 0)),
        pl.BlockSpec((None, mod_rows, D), lambda i, j: (layer, 0, 4)),
        pl.BlockSpec((None, mod_rows, D), lambda i, j: (layer, 0, 3)),
        pl.BlockSpec((None, D, FF_TN), lambda i, j: (layer, 0, j)),
        pl.BlockSpec((None, D, FF_TN), lambda i, j: (layer, 0, FF_NJ + j)),
        pl.BlockSpec((None, FFN_K, FF_TN), lambda i, j: (layer, 0, j)),
        pl.BlockSpec((None, FFN_K, FF_TN), lambda i, j: (layer, 0, FF_NJ + j)),
        pl.BlockSpec((None, 1, FF_TN), lambda i, j: (layer, 0, j)),
        pl.BlockSpec((None, 1, FF_TN), lambda i, j: (layer, 0, FF_NJ + j)),
    ]
    args = [x, norm_g, mod, mod, w, w, cw, cw, cb, cb]
    scratch = [pltpu.VMEM((tm, D), BF16), pltpu.VMEM((pre + tm, 2 * FF_TN), F32)]
    if prompt:
        st_spec = pl.BlockSpec((None, FFN_K - 1, FF_TN), lambda i, j: (i, 0, j))
        st_shape = jax.ShapeDtypeStruct((m // tm, FFN_K - 1, D_FF), F32)
        scratch.append(pltpu.VMEM((FF_NJ, pre, 2 * FF_TN), F32))
    else:
        for half in range(2):
            in_specs += [pl.BlockSpec((None, None, BS, FF_TN),
                                      lambda i, j, k=k, half=half: (layer, k, 0, half * FF_NJ + j))
                         for k in range(FFN_K - 1)]
            args += [st_in] * (FFN_K - 1)
        st_spec = pl.BlockSpec((FFN_K - 1, BS, FF_TN), lambda i, j: (0, 0, j))
        st_shape = jax.ShapeDtypeStruct((FFN_K - 1, BS, D_FF), F32)
    return pl.pallas_call(
        kern,
        grid=(m // tm, FF_NJ),
        in_specs=in_specs,
        out_specs=[pl.BlockSpec((tm, FF_TN), lambda i, j: (i, j)), st_spec, st_spec],
        out_shape=[jax.ShapeDtypeStruct((m, D_FF), BF16), st_shape, st_shape],
        scratch_shapes=scratch,
        compiler_params=_cparams(("arbitrary", "arbitrary")),
        name="ffn_up_p" if prompt else "ffn_up_s",
    )(*args)


def _final_kernel(x_ref, g_ref, o_ref):
    x = x_ref[...]
    ms = jnp.mean(x * x, axis=-1, keepdims=True)
    o_ref[...] = (x * lax.rsqrt(ms + EPS)) * g_ref[...]


def _final_call(x, g):
    m = x.shape[0]
    tm = 256
    return pl.pallas_call(
        _final_kernel,
        grid=(m // tm,),
        in_specs=[pl.BlockSpec((tm, D), lambda i: (i, 0)), pl.BlockSpec((1, D), lambda i: (0, 0))],
        out_specs=pl.BlockSpec((tm, D), lambda i: (i, 0)),
        out_shape=jax.ShapeDtypeStruct((m, D), F32),
        compiler_params=_cparams(("arbitrary",)),
        name="final_norm",
    )(x, g)


def _split3(x):
    x1 = x.astype(BF16)
    r = x - x1.astype(F32)
    x2 = r.astype(BF16)
    r = r - x2.astype(F32)
    return x1, x2, r.astype(BF16)


def _ret_log_gamma(h):
    return math.log1p(-(2.0 ** (-5.0 - h)))


def _lru_gates(xc, wa_ref, ba_ref, wx_ref, bx_ref, lam_ref):
    xb = xc.astype(BF16)
    r = _sigmoid(_dot(xb, wa_ref[...]) + ba_ref[...])
    i = _sigmoid(_dot(xb, wx_ref[...]) + bx_ref[...])
    log_a = (-LRU_C * r) * _softplus(-lam_ref[...])
    a = jnp.exp(log_a)
    u = jnp.sqrt(-jnp.tanh(log_a) * (a * a + 1.0)) * (i * xc)
    return a, u


MIX_TC = 256


def _mixer_prompt_kernel(z_ref, cos_ref, sin_ref, wal_ref, bal_ref, gng_ref, lcw_ref, lcb_ref,
                         wa_ref, ba_ref, wx_ref, bx_ref, lam_ref, rng_ref,
                         mix_ref, sg_ref, sr_ref, hl_ref, lc_ref,
                         stg_ref, str_ref, h_ref, lxe_ref, a_ref, u_ref, hs_ref, att_s):
    t_id = pl.program_id(1)
    nt = pl.num_programs(1)
    tc = MIX_TC
    pre = SUBLANES

    @pl.when(t_id == 0)
    def _():
        stg_ref[...] = jnp.zeros_like(stg_ref)
        str_ref[...] = jnp.zeros_like(str_ref)
        h_ref[...] = jnp.zeros_like(h_ref)
        lxe_ref[0:pre, :] = jnp.zeros((pre, W_LRU), F32)

    lxe_ref[pre:pre + tc, :] = z_ref[:, LX0:LX0 + W_LRU]

    row_c = lax.broadcasted_iota(jnp.int32, (CHUNK, CHUNK), 0)
    col_c = lax.broadcasted_iota(jnp.int32, (CHUNK, CHUNK), 1)
    tri = (col_c <= row_c).astype(BF16)
    rel = (row_c - col_c).astype(F32)
    rowf = lax.broadcasted_iota(jnp.int32, (CHUNK, LANES), 0).astype(F32)
    row1 = lax.broadcasted_iota(jnp.int32, (CHUNK, 1), 0)
    sub_row = lax.broadcasted_iota(jnp.int32, (SUBLANES, 1), 0)
    lane_id = lax.broadcasted_iota(jnp.int32, (SUBLANES, LANES), 1)

    def chunk_body(c, carry):
        r0 = pl.multiple_of(c * CHUNK, CHUNK)
        rows = pl.ds(r0, CHUNK)

        zq = z_ref[rows, Q0:Q0 + H_GLA * DKP] * (DK_GLA ** -0.5)
        zk = z_ref[rows, K0:K0 + H_GLA * DKP]
        lr = z_ref[rows, LR0:LR0 + LANES].astype(BF16)
        la = _log_sigmoid(_dot(lr, wal_ref[...]) + bal_ref[...]) * (1.0 / GLA_TAU)
        p1, p2, p3 = _split3(la)
        bcs = _dot(tri, p1) + _dot(tri, p2) + _dot(tri, p3)
        b_last = bcs[CHUNK - 1:CHUNK, :]
        qb = (zq * jnp.exp(bcs)).astype(BF16)
        kb = (zk * jnp.exp(b_last - bcs)).astype(BF16)
        e_last = jnp.exp(b_last)

        bcs2 = bcs * LOG2E
        for i in range(CHUNK // SUB):
            blk = slice(i * SUB, (i + 1) * SUB)
            q_blk = zq[blk, :]
            b_blk = bcs[blk, :]
            b2_blk = bcs2[blk, :]
            if i > 0:
                r_i = bcs[i * SUB - 1:i * SUB, :]
                q_i = (q_blk * jnp.exp(b_blk - r_i)).astype(BF16)
                kk = (zk * jnp.exp(jnp.where(row1 < i * SUB, r_i - bcs, NEG))).astype(BF16)

            att = [[jnp.zeros((SUBLANES, LANES), F32) for _ in range(SUB // SUBLANES)] for _ in range(H_GLA)]
            for sl in range(SUB):
                s = i * SUB + sl
                for g in range(sl // SUBLANES, SUB // SUBLANES):
                    rows_g = slice(g * SUBLANES, (g + 1) * SUBLANES)
                    diff = b2_blk[rows_g, :] - bcs2[s:s + 1, :]
                    if sl > g * SUBLANES:
                        diff = jnp.where(sub_row >= sl - g * SUBLANES, diff, NEG)
                    w = q_blk[rows_g, :] * zk[s:s + 1, :] * jnp.exp2(diff)
                    for h in range(H_GLA):
                        col = jnp.sum(w[:, h * DKP:(h + 1) * DKP], axis=-1, keepdims=True)
                        att[h][g] = jnp.where(lane_id == s, col, att[h][g])
            for h in range(H_GLA):
                a_h = jnp.concatenate(att[h], axis=0)[:, :CHUNK]
                if i > 0:
                    a_h = a_h + _dot_nt(q_i[:, h * DKP:(h + 1) * DKP], kk[:, h * DKP:(h + 1) * DKP])
                att_s[h, blk, 0:CHUNK] = a_h

        for h in range(H_GLA):
            v_h = z_ref[rows, V0 + h * DVP:V0 + (h + 1) * DVP].astype(BF16)
            st_old = stg_ref[h]
            o_h = _dot(att_s[h, :, 0:CHUNK].astype(BF16), v_h)
            o_h = o_h + _dot_nt(qb[:, h * DKP:(h + 1) * DKP], st_old.astype(BF16))
            stg_ref[h] = e_last[:, h * DKP:(h + 1) * DKP] * st_old + _dot_tn(v_h, kb[:, h * DKP:(h + 1) * DKP])
            gate = z_ref[rows, GG0 + h * DVP:GG0 + (h + 1) * DVP]
            y = _head_norm(o_h, DV_GLA) * gng_ref[:, h * DVP:(h + 1) * DVP] * _silu(gate)
            mix_ref[rows, MG0 + h * DVP:MG0 + (h + 1) * DVP] = y.astype(BF16)

        cosv = cos_ref[rows, :]
        sinv = sin_ref[rows, :]
        for h in range(H_RET):
            lg = _ret_log_gamma(h)
            cols = slice(h * DK_RET, (h + 1) * DK_RET)
            xq = z_ref[rows, RQ0 + h * DK_RET:RQ0 + (h + 1) * DK_RET]
            xk = z_ref[rows, RK0 + h * DK_RET:RK0 + (h + 1) * DK_RET]
            q = (xq * cosv + pltpu.roll(xq, DK_RET // 2, axis=1) * sinv) * (DK_RET ** -0.5)
            k = xk * cosv + pltpu.roll(xk, DK_RET // 2, axis=1) * sinv
            v = z_ref[rows, RV0 + h * DK_RET:RV0 + (h + 1) * DK_RET].astype(BF16)
            decay = jnp.where(rel >= 0, jnp.exp(jnp.maximum(rel, 0.0) * lg), 0.0)
            q_dec = jnp.exp((rowf + 1.0) * lg)
            k_dec = jnp.exp((CHUNK - 1.0 - rowf) * lg)
            c_dec = math.exp(CHUNK * lg)
            qb16 = q.astype(BF16)
            att_r = _dot_nt(qb16, k.astype(BF16)) * decay
            s_old = str_ref[h]
            o = _dot(att_r.astype(BF16), v) + _dot(qb16, s_old.astype(BF16)) * q_dec
            str_ref[h] = c_dec * s_old + _dot_tn((k * k_dec).astype(BF16), v)
            gate = z_ref[rows, RG0 + h * DK_RET:RG0 + (h + 1) * DK_RET]
            y = _head_norm(o, DK_RET) * rng_ref[:, cols] * _silu(gate)
            mix_ref[rows, MR0 + h * DK_RET:MR0 + (h + 1) * DK_RET] = y.astype(BF16)

        return carry

    lax.fori_loop(0, tc // CHUNK, chunk_body, 0, unroll=2)

    cw = lcw_ref[...]
    xc = lcb_ref[...] + cw[0:1, :] * lxe_ref[pre - 3:pre - 3 + tc, :]
    xc = xc + cw[1:2, :] * lxe_ref[pre - 2:pre - 2 + tc, :]
    xc = xc + cw[2:3, :] * lxe_ref[pre - 1:pre - 1 + tc, :]
    xc = xc + cw[3:4, :] * lxe_ref[pre:pre + tc, :]
    a, u = _lru_gates(xc, wa_ref, ba_ref, wx_ref, bx_ref, lam_ref)
    a_ref[...] = a
    u_ref[...] = u

    def scan_body(t, h):
        h = a_ref[pl.ds(t, 1), :] * h + u_ref[pl.ds(t, 1), :]
        hs_ref[pl.ds(t, 1), :] = h
        return h

    h_fin = lax.fori_loop(0, tc, scan_body, h_ref[0:1, :], unroll=8)
    h_ref[0:1, :] = h_fin

    def out_body(c, carry):
        rows = pl.ds(pl.multiple_of(c * CHUNK, CHUNK), CHUNK)
        y = hs_ref[rows, :] * _gelu_tanh(z_ref[rows, LG0:LG0 + W_LRU])
        mix_ref[rows, ML0:ML0 + W_LRU] = y.astype(BF16)
        return carry

    lax.fori_loop(0, tc // CHUNK, out_body, 0)

    lxe_ref[0:pre, :] = lxe_ref[tc:tc + pre, :]

    @pl.when(t_id == nt - 1)
    def _():
        for h in range(H_GLA):
            sg_ref[h] = stg_ref[h].T[:DK_GLA, :DV_GLA]
        sr_ref[...] = str_ref[...]
        hl_ref[...] = h_fin
        lc_ref[...] = lxe_ref[tc + pre - (LRU_K - 1):tc + pre, :]


def _mixer_prompt_call(z, cos_t, sin_t, pw, *, layer):
    tc = MIX_TC
    nt = TP // tc

    def lspec(shape):
        return pl.BlockSpec((None,) + shape, lambda b, t: (layer,) + (0,) * len(shape))

    in_specs = [
        pl.BlockSpec((tc, NZ), lambda b, t: (b * nt + t, 0)),
        pl.BlockSpec((tc, LANES), lambda b, t: (t, 0)),
        pl.BlockSpec((tc, LANES), lambda b, t: (t, 0)),
        lspec((LANES, H_GLA * DKP)), lspec((1, H_GLA * DKP)), lspec((1, H_GLA * DVP)),
        lspec((LRU_K, W_LRU)), lspec((1, W_LRU)),
        lspec((W_LRU, W_LRU)), lspec((1, W_LRU)), lspec((W_LRU, W_LRU)), lspec((1, W_LRU)), lspec((1, W_LRU)),
        lspec((1, H_RET * DK_RET)),
    ]
    out_specs = [
        pl.BlockSpec((tc, NMIX), lambda b, t: (b * nt + t, 0)),
        pl.BlockSpec((None, H_GLA, DK_GLA, DV_GLA), lambda b, t: (b, 0, 0, 0)),
        pl.BlockSpec((None, H_RET, DK_RET, DK_RET), lambda b, t: (b, 0, 0, 0)),
        pl.BlockSpec((None, 1, W_LRU), lambda b, t: (b, 0, 0)),
        pl.BlockSpec((None, LRU_K - 1, W_LRU), lambda b, t: (b, 0, 0)),
    ]
    out_shape = [
        jax.ShapeDtypeStruct((BP * TP, NMIX), BF16),
        jax.ShapeDtypeStruct((BP, H_GLA, DK_GLA, DV_GLA), F32),
        jax.ShapeDtypeStruct((BP, H_RET, DK_RET, DK_RET), F32),
        jax.ShapeDtypeStruct((BP, 1, W_LRU), F32),
        jax.ShapeDtypeStruct((BP, LRU_K - 1, W_LRU), F32),
    ]
    scratch = [
        pltpu.VMEM((H_GLA, DVP, DKP), F32),
        pltpu.VMEM((H_RET, DK_RET, DK_RET), F32),
        pltpu.VMEM((SUBLANES, W_LRU), F32),
        pltpu.VMEM((SUBLANES + tc, W_LRU), F32),
        pltpu.VMEM((tc, W_LRU), F32), pltpu.VMEM((tc, W_LRU), F32), pltpu.VMEM((tc, W_LRU), F32),
        pltpu.VMEM((H_GLA, CHUNK, LANES), F32),
    ]
    return pl.pallas_call(
        _mixer_prompt_kernel,
        grid=(BP, nt),
        in_specs=in_specs,
        out_specs=out_specs,
        out_shape=out_shape,
        scratch_shapes=scratch,
        compiler_params=_cparams(("arbitrary", "arbitrary")),
        name="mixer_p",
    )(z, cos_t, sin_t, pw["w_alpha"], pw["b_alpha"], pw["gla_norm_g"], pw["lru_conv_w"], pw["lru_conv_b"],
      pw["lru_wa"], pw["lru_ba"], pw["lru_wx"], pw["lru_bx"], pw["lru_lam"], pw["ret_norm_g"])


MIX_BB = 8


def _mixer_sample_kernel(z_ref, cos_ref, sin_ref, wal_ref, bal_ref, gng_ref, lcw_ref, lcb_ref,
                         wa_ref, ba_ref, wx_ref, bx_ref, lam_ref, rng_ref,
                         sg0_ref, sr0_ref, h0_ref, lc0_ref,
                         mix_ref, sg_ref, sr_ref, hl_ref, lc_ref,
                         qb_s, kb_s, v_s, og_s, rq_s, rk_s, rv_s, or_s, sp_ref, et_ref):
    bb = MIX_BB
    nrow = TS * bb

    @pl.when(pl.program_id(0) == 0)
    def _():
        sp_ref[...] = jnp.zeros_like(sp_ref)
        et_ref[...] = jnp.zeros_like(et_ref)

    def zcols(c0, width):
        return z_ref[:, :, c0:c0 + width].reshape(nrow, width)

    def slab(x, t):
        return x[t * bb:(t + 1) * bb, :]

    zq = zcols(Q0, H_GLA * DKP) * (DK_GLA ** -0.5)
    zk = zcols(K0, H_GLA * DKP)
    zv = zcols(V0, H_GLA * DVP)
    la = _log_sigmoid(_dot(zcols(LR0, LANES).astype(BF16), wal_ref[...]) + bal_ref[...]) * (1.0 / GLA_TAU)
    bs = [slab(la, 0)]
    for t in range(1, TS):
        bs.append(bs[-1] + slab(la, t))
    b_last = bs[-1]
    e_last = jnp.exp(b_last)
    o_t = []
    for t in range(TS):
        acc = None
        for s in range(t + 1):
            w = slab(zq, t) * slab(zk, s) * jnp.exp(bs[t] - bs[s])
            parts = []
            for h in range(H_GLA):
                a_ts = jnp.sum(w[:, h * DKP:(h + 1) * DKP], axis=-1, keepdims=True)
                parts.append(a_ts * slab(zv, s)[:, h * DVP:(h + 1) * DVP])
            contrib = jnp.concatenate(parts, axis=1)
            acc = contrib if acc is None else acc + contrib
        o_t.append(acc)
    o_intra = jnp.concatenate(o_t, axis=0)
    b_all = jnp.concatenate(bs, axis=0)
    qb = zq * jnp.exp(b_all)
    kb = zk * jnp.exp(jnp.concatenate([b_last] * TS, axis=0) - b_all)
    for h in range(H_GLA):
        qb_s[h] = qb[:, h * DKP:(h + 1) * DKP]
        kb_s[h] = kb[:, h * DKP:(h + 1) * DKP]
        for p in range(DVP // LANES):
            v_s[h * (DVP // LANES) + p] = zv[:, h * DVP + p * LANES:h * DVP + (p + 1) * LANES]
        et_ref[h, 0:bb, :] = e_last[:, h * DKP:(h + 1) * DKP]
    e_cols = [et_ref[h].T for h in range(H_GLA)]

    for b in range(bb):
        seq = pl.ds(b, TS, stride=bb)
        for h in range(H_GLA):
            s0 = sg0_ref[b, h]
            sp_ref[0:DK_GLA, 0:DV_GLA] = s0
            q_bh = qb_s[h, seq, :]
            k_bh = kb_s[h, seq, :]
            v_bh = jnp.concatenate([v_s[h * (DVP // LANES) + p, seq, :] for p in range(DVP // LANES)], axis=1)
            o_bh = _dot(q_bh, sp_ref[...])
            for p in range(DVP // LANES):
                og_s[h * (DVP // LANES) + p, seq, :] = o_bh[:, p * LANES:(p + 1) * LANES]
            upd = _dot_tn(k_bh, v_bh)
            s_new = e_cols[h][:, b:b + 1] * sp_ref[...] + upd
            sg_ref[b, h] = s_new[:DK_GLA, :DV_GLA]

    og = jnp.concatenate([og_s[i] for i in range(H_GLA * DVP // LANES)], axis=1) + o_intra
    for h in range(H_GLA):
        cols = slice(h * DVP, (h + 1) * DVP)
        y = _head_norm(og[:, cols], DV_GLA) * gng_ref[:, cols] * _silu(zcols(GG0 + h * DVP, DVP))
        mix_ref[:, :, MG0 + h * DVP:MG0 + (h + 1) * DVP] = y.reshape(TS, bb, DVP)

    cosv = jnp.concatenate([jnp.broadcast_to(cos_ref[t:t + 1, :], (bb, LANES)) for t in range(TS)], axis=0)
    sinv = jnp.concatenate([jnp.broadcast_to(sin_ref[t:t + 1, :], (bb, LANES)) for t in range(TS)], axis=0)
    trow = lax.broadcasted_iota(jnp.int32, (TS, LANES), 0).astype(F32)
    r_intra = []
    for h in range(H_RET):
        lg = _ret_log_gamma(h)
        xq = zcols(RQ0 + h * DK_RET, DK_RET)
        xk = zcols(RK0 + h * DK_RET, DK_RET)
        q = (xq * cosv + pltpu.roll(xq, DK_RET // 2, axis=1) * sinv) * (DK_RET ** -0.5)
        k = xk * cosv + pltpu.roll(xk, DK_RET // 2, axis=1) * sinv
        v = zcols(RV0 + h * DK_RET, DK_RET)
        rq_s[h] = q
        rk_s[h] = k
        rv_s[h] = v
        outs = []
        for t in range(TS):
            acc = None
            for s in range(t + 1):
                a_ts = jnp.sum(slab(q, t) * slab(k, s), axis=-1, keepdims=True) * math.exp((t - s) * lg)
                contrib = a_ts * slab(v, s)
                acc = contrib if acc is None else acc + contrib
            outs.append(acc)
        r_intra.append(jnp.concatenate(outs, axis=0))

    for b in range(bb):
        seq = pl.ds(b, TS, stride=bb)
        for h in range(H_RET):
            lg = _ret_log_gamma(h)
            q_dec = jnp.exp((trow + 1.0) * lg)
            k_dec = jnp.exp((TS - 1.0 - trow) * lg)
            s0 = sr0_ref[b, h]
            q_bh = rq_s[h, seq, :]
            k_bh = rk_s[h, seq, :]
            v_bh = rv_s[h, seq, :]
            or_s[h, seq, :] = _dot(q_bh, s0) * q_dec
            sr_ref[b, h] = math.exp(TS * lg) * s0 + _dot_tn(k_bh * k_dec, v_bh)

    for h in range(H_RET):
        cols = slice(h * DK_RET, (h + 1) * DK_RET)
        o = or_s[h] + r_intra[h]
        y = _head_norm(o, DK_RET) * rng_ref[:, cols] * _silu(zcols(RG0 + h * DK_RET, DK_RET))
        mix_ref[:, :, MR0 + h * DK_RET:MR0 + (h + 1) * DK_RET] = y.reshape(TS, bb, DK_RET)

    lx = zcols(LX0, W_LRU)
    xe = [lc0_ref[:, k * W_LRU:(k + 1) * W_LRU] for k in range(LRU_K - 1)] + [slab(lx, t) for t in range(TS)]
    cw = lcw_ref[...]
    xc = []
    for t in range(TS):
        acc = lcb_ref[...] + cw[0:1, :] * xe[t]
        for k in range(1, LRU_K):
            acc = acc + cw[k:k + 1, :] * xe[t + k]
        xc.append(acc)
    a, u = _lru_gates(jnp.concatenate(xc, axis=0), wa_ref, ba_ref, wx_ref, bx_ref, lam_ref)
    h = h0_ref[...]
    hs = []
    for t in range(TS):
        h = slab(a, t) * h + slab(u, t)
        hs.append(h)
    y = jnp.concatenate(hs, axis=0) * _gelu_tanh(zcols(LG0, W_LRU))
    mix_ref[:, :, ML0:ML0 + W_LRU] = y.reshape(TS, bb, W_LRU)
    hl_ref[...] = h
    for k in range(LRU_K - 1):
        lc_ref[:, k * W_LRU:(k + 1) * W_LRU] = xe[TS + k]


def _mixer_sample_call(z, cos_t, sin_t, pw, st_gla, st_ret, st_lru, st_lconv, *, layer):
    bb = MIX_BB

    def lspec(shape):
        return pl.BlockSpec((None,) + shape, lambda j: (layer,) + (0,) * len(shape))

    in_specs = [
        pl.BlockSpec((TS, bb, NZ), lambda j: (0, j, 0)),
        pl.BlockSpec((TS, LANES), lambda j: (0, 0)),
        pl.BlockSpec((TS, LANES), lambda j: (0, 0)),
        lspec((LANES, H_GLA * DKP)), lspec((1, H_GLA * DKP)), lspec((1, H_GLA * DVP)),
        lspec((LRU_K, W_LRU)), lspec((1, W_LRU)),
        lspec((W_LRU, W_LRU)), lspec((1, W_LRU)), lspec((W_LRU, W_LRU)), lspec((1, W_LRU)), lspec((1, W_LRU)),
        lspec((1, H_RET * DK_RET)),
        pl.BlockSpec((None, bb, H_GLA, DK_GLA, DV_GLA), lambda j: (layer, j, 0, 0, 0)),
        pl.BlockSpec((None, bb, H_RET, DK_RET, DK_RET), lambda j: (layer, j, 0, 0, 0)),
        pl.BlockSpec((None, bb, W_LRU), lambda j: (layer, j, 0)),
        pl.BlockSpec((None, bb, (LRU_K - 1) * W_LRU), lambda j: (layer, j, 0)),
    ]
    out_specs = [
        pl.BlockSpec((TS, bb, NMIX), lambda j: (0, j, 0)),
        pl.BlockSpec((bb, H_GLA, DK_GLA, DV_GLA), lambda j: (j, 0, 0, 0)),
        pl.BlockSpec((bb, H_RET, DK_RET, DK_RET), lambda j: (j, 0, 0, 0)),
        pl.BlockSpec((bb, W_LRU), lambda j: (j, 0)),
        pl.BlockSpec((bb, (LRU_K - 1) * W_LRU), lambda j: (j, 0)),
    ]
    out_shape = [
        jax.ShapeDtypeStruct((TS, BS, NMIX), F32),
        jax.ShapeDtypeStruct((BS, H_GLA, DK_GLA, DV_GLA), F32),
        jax.ShapeDtypeStruct((BS, H_RET, DK_RET, DK_RET), F32),
        jax.ShapeDtypeStruct((BS, W_LRU), F32),
        jax.ShapeDtypeStruct((BS, (LRU_K - 1) * W_LRU), F32),
    ]
    nrow = TS * bb
    scratch = [
        pltpu.VMEM((H_GLA, nrow, LANES), F32), pltpu.VMEM((H_GLA, nrow, LANES), F32),
        pltpu.VMEM((H_GLA * DVP // LANES, nrow, LANES), F32), pltpu.VMEM((H_GLA * DVP // LANES, nrow, LANES), F32),
        pltpu.VMEM((H_RET, nrow, LANES), F32), pltpu.VMEM((H_RET, nrow, LANES), F32),
        pltpu.VMEM((H_RET, nrow, LANES), F32), pltpu.VMEM((H_RET, nrow, LANES), F32),
        pltpu.VMEM((DKP, DVP), F32),
        pltpu.VMEM((H_GLA, LANES, LANES), F32),
    ]
    return pl.pallas_call(
        _mixer_sample_kernel,
        grid=(BS // bb,),
        in_specs=in_specs,
        out_specs=out_specs,
        out_shape=out_shape,
        scratch_shapes=scratch,
        compiler_params=_cparams(("arbitrary",)),
        name="mixer_s",
    )(z, cos_t, sin_t, pw["w_alpha"], pw["b_alpha"], pw["gla_norm_g"], pw["lru_conv_w"], pw["lru_conv_b"],
      pw["lru_wa"], pw["lru_ba"], pw["lru_wx"], pw["lru_bx"], pw["lru_lam"], pw["ret_norm_g"],
      st_gla, st_ret, st_lru, st_lconv)


def _pad_heads(w, heads, d, dp):
    lead = w.shape[:-1]
    w = w.reshape(lead + (heads, d))
    w = jnp.pad(w, [(0, 0)] * len(lead) + [(0, 0), (0, dp - d)])
    return w.reshape(lead + (heads * dp,))


def _split_cols(w, sizes):
    out, off = [], 0
    for s in sizes:
        out.append(w[..., off:off + s])
        off += s
    return out


def _rope_tables(start, length):
    half = DK_RET // 2
    freqs = ROPE_BASE ** (-jnp.arange(half, dtype=F32) / half)
    pos = start + jnp.arange(length, dtype=jnp.int32)
    ang = pos.astype(F32)[:, None] * freqs[None, :]
    cos, sin = jnp.cos(ang), jnp.sin(ang)
    return jnp.concatenate([cos, cos], axis=1), jnp.concatenate([-sin, sin], axis=1)


def _prep_weights(w_in, gla_w_alpha, gla_b_alpha, gla_norm_g, lru_conv_w, lru_conv_b, lru_w_a, lru_b_a,
                  lru_w_x, lru_b_x, lru_lambda, ret_norm_g, w_out, ffn_w_up, ffn_conv_w, ffn_conv_b, ffn_w_down):
    w_gla, w_ret = H_GLA * DV_GLA, H_RET * DK_RET
    sizes = (H_GLA * DK_GLA, H_GLA * DK_GLA, w_gla, GLA_RANK, w_gla, W_LRU, W_LRU, w_ret, w_ret, w_ret, w_ret)
    w_rows = jnp.swapaxes(w_in, 1, 2).astype(BF16)
    pieces, off = [], 0

    def take(n_rows):
        nonlocal off
        blk = w_rows[:, off:off + n_rows]
        off += n_rows
        return blk

    def zeros(n_rows):
        return jnp.zeros((DEPTH, n_rows, D), BF16)

    for heads, d, dp in ((H_GLA, DK_GLA, DKP), (H_GLA, DK_GLA, DKP), (H_GLA, DV_GLA, DVP), (1, GLA_RANK, LANES),
                         (H_GLA, DV_GLA, DVP)):
        for _ in range(heads):
            pieces += [take(d), zeros(dp - d)]
    pieces += [take(2 * W_LRU + 4 * w_ret), zeros(NZ - NZ_USED)]
    w_in_p = jnp.concatenate(pieces, axis=1)
    w_alpha = jnp.pad(_pad_heads(gla_w_alpha, H_GLA, DK_GLA, DKP), ((0, 0), (0, LANES - GLA_RANK), (0, 0))).astype(BF16)
    eye = jnp.eye(H_LRU, dtype=F32)

    def block_diag(w):
        return jnp.einsum("lhij,hg->lhigj", w, eye).reshape(DEPTH, W_LRU, W_LRU).astype(BF16)

    wo_g, wo_l, wo_r = _split_cols(jnp.swapaxes(w_out, 1, 2), (w_gla, W_LRU, w_ret))
    w_out_p = jnp.swapaxes(jnp.concatenate([_pad_heads(wo_g, H_GLA, DV_GLA, DVP), wo_l, wo_r], axis=-1), 1, 2)
    return dict(
        w_in=w_in_p,
        w_alpha=w_alpha,
        b_alpha=_pad_heads(gla_b_alpha, H_GLA, DK_GLA, DKP)[:, None, :],
        gla_norm_g=_pad_heads(gla_norm_g, H_GLA, DV_GLA, DVP)[:, None, :],
        lru_conv_w=lru_conv_w,
        lru_conv_b=lru_conv_b[:, None, :],
        lru_wa=block_diag(lru_w_a), lru_ba=lru_b_a[:, None, :],
        lru_wx=block_diag(lru_w_x), lru_bx=lru_b_x[:, None, :],
        lru_lam=lru_lambda[:, None, :],
        ret_norm_g=ret_norm_g[:, None, :],
        w_out=w_out_p.astype(BF16),
        w_up=ffn_w_up.astype(BF16),
        ffn_cw=ffn_conv_w,
        ffn_cb=ffn_conv_b[:, None, :],
        w_down=ffn_w_down.astype(BF16),
    )


def _run_group(x, mod, pw, norm1_g, norm2_g, tables, states, *, prompt):
    tm = 1024
    tn_in, tn_out, tn_down = 1024, 1024, 512
    outs = []
    for l in range(DEPTH):
        z = _proj_call(x, norm1_g, mod, pw["w_in"], layer=l, sc_chunk=1, sh_chunk=0, prompt=prompt, tm=tm, tn=tn_in)
        if prompt:
            mixed, s_gla, s_ret, s_lru, s_lconv = _mixer_prompt_call(z, *tables, pw, layer=l)
            s_lru = s_lru.reshape(BP, W_LRU)
        else:
            st_gla, st_ret, st_lru, st_lconv, _ = states
            mixed, s_gla, s_ret, s_lru, s_lconv = _mixer_sample_call(
                z.reshape(TS, BS, NZ), *tables, pw, st_gla, st_ret, st_lru, st_lconv, layer=l)
            mixed = mixed.reshape(TS * BS, NMIX)
            s_lconv = s_lconv.reshape(BS, LRU_K - 1, W_LRU)
        x = _resid_call(mixed, pw["w_out"], x, mod, layer=l, g_chunk=2, prompt=prompt, tm=tm, tn=tn_out,
                        name="out_proj_p" if prompt else "out_proj_s")
        act, s_fg, s_fv = _ffn_up_call(x, norm2_g, mod, pw["w_up"], pw["ffn_cw"], pw["ffn_cb"],
                                       None if prompt else states[4], layer=l, prompt=prompt, tm=tm)
        x = _resid_call(act, pw["w_down"], x, mod, layer=l, g_chunk=5, prompt=prompt, tm=tm, tn=tn_down,
                        name="ffn_down_p" if prompt else "ffn_down_s")
        s_fconv = jnp.concatenate([s_fg, s_fv], axis=-1)
        if prompt:
            s_fconv = s_fconv[TP // tm - 1::TP // tm]
        else:
            s_fconv = jnp.swapaxes(s_fconv, 0, 1)
        outs.append((s_gla, s_ret, s_lru, s_lconv, s_fconv))
    return x, [jnp.stack(o) for o in zip(*outs)]


def kernel(x_prompt, x_sample, state_gla, state_ret, state_lru, state_lru_conv, state_ffn_conv, c_prompt, c_sample,
           norm1_g, norm2_g, final_g, w_ada, b_ada, w_in, gla_w_alpha, gla_b_alpha, gla_norm_g, lru_conv_w,
           lru_conv_b, lru_w_a, lru_b_a, lru_w_x, lru_b_x, lru_lambda, ret_norm_g, w_out, ffn_w_up, ffn_conv_w,
           ffn_conv_b, ffn_w_down):
    pw = _prep_weights(w_in, gla_w_alpha, gla_b_alpha, gla_norm_g, lru_conv_w, lru_conv_b, lru_w_a, lru_b_a,
                       lru_w_x, lru_b_x, lru_lambda, ret_norm_g, w_out, ffn_w_up, ffn_conv_w, ffn_conv_b, ffn_w_down)
    cp8 = jnp.pad(c_prompt, ((0, SUBLANES - BP), (0, 0)))
    mod_p, mod_s = _mod_call(cp8, c_sample, w_ada, b_ada[:, None, :])
    n1 = norm1_g[:, None, :]
    n2 = norm2_g[:, None, :]

    xp = x_prompt.reshape(BP * TP, D)
    xs = jnp.swapaxes(x_sample, 0, 1).reshape(TS * BS, D)
    st_s = (state_gla, state_ret, state_lru, state_lru_conv.reshape(DEPTH, BS, (LRU_K - 1) * W_LRU),
            jnp.swapaxes(state_ffn_conv, 1, 2))

    xp, outs_p = _run_group(xp, mod_p, pw, n1, n2, _rope_tables(0, TP), None, prompt=True)
    xs, outs_s = _run_group(xs, mod_s, pw, n1, n2, _rope_tables(PAST, TS), st_s, prompt=False)

    fg = final_g[None, :]
    y_p = _final_call(xp, fg).reshape(BP, TP, D)
    y_s = jnp.swapaxes(_final_call(xs, fg).reshape(TS, BS, D), 0, 1)
    return (y_p, y_s, *outs_p, *outs_s)
```

```python
import functools
import math

import jax
import jax.numpy as jnp
from jax import lax
from jax.experimental import pallas as pl
from jax.experimental.pallas import tpu as pltpu

F32 = jnp.float32
BF16 = jnp.bfloat16

D = 2048
DEPTH = 4
BP, TP = 4, 2048
BS, TS = 128, 8
PAST = 16384
H_GLA, DK_GLA, DV_GLA = 4, 96, 192
GLA_RANK = 16
GLA_TAU = 16.0
W_LRU, H_LRU, BLK_LRU = 768, 8, 96
LRU_K = 4
LRU_C = 8.0
H_RET, DK_RET = 4, 128
ROPE_BASE = 10000.0
D_FF = 5632
FFN_K = 3
CHUNK = 64
EPS = 1e-6

LANES = 128
SUBLANES = 8
DKP, DVP = 128, 256

Q0 = 0
K0 = Q0 + H_GLA * DKP
V0 = K0 + H_GLA * DKP
LR0 = V0 + H_GLA * DVP
GG0 = LR0 + LANES
LX0 = GG0 + H_GLA * DVP
LG0 = LX0 + W_LRU
RQ0 = LG0 + W_LRU
RK0 = RQ0 + H_RET * DK_RET
RV0 = RK0 + H_RET * DK_RET
RG0 = RV0 + H_RET * DK_RET
NZ_USED = RG0 + H_RET * DK_RET
NZ = -(-NZ_USED // 1024) * 1024
MG0 = 0
ML0 = H_GLA * DVP
MR0 = ML0 + W_LRU
NMIX = MR0 + H_RET * DK_RET

SUB = 16
NEG = -1e30
LOG2E = math.log2(math.e)
FF_TN = 512
FF_NJ = D_FF // FF_TN

VMEM_LIMIT = 56 * 1024 * 1024


def _cparams(sem, flags=None):
    return pltpu.CompilerParams(dimension_semantics=sem, vmem_limit_bytes=VMEM_LIMIT, flags=flags)


def _sigmoid(x):
    return 1.0 / (1.0 + jnp.exp(-x))


def _silu(x):
    return x * _sigmoid(x)


def _gelu_tanh(x):
    c = math.sqrt(2.0 / math.pi)
    return 0.5 * x * (1.0 + jnp.tanh(c * (x + 0.044715 * (x * x * x))))


def _log_sigmoid(x):
    return jnp.minimum(x, 0.0) - jnp.log1p(jnp.exp(-jnp.abs(x)))


def _softplus(x):
    return jnp.maximum(x, 0.0) + jnp.log1p(jnp.exp(-jnp.abs(x)))


def _dot(a, b):
    return jnp.dot(a, b, preferred_element_type=F32)


def _dot_nt(a, b):
    return lax.dot_general(a, b, (((1,), (1,)), ((), ())), preferred_element_type=F32)


def _dot_tn(a, b):
    return lax.dot_general(a, b, (((0,), (0,)), ((), ())), preferred_element_type=F32)


def _norm_mod(x, g, sc, sh):
    ms = jnp.mean(x * x, axis=-1, keepdims=True)
    return (x * lax.rsqrt(ms + EPS)) * g * (1.0 + sc) + sh


def _head_norm(o, width):
    ms = jnp.sum(o * o, axis=-1, keepdims=True) * (1.0 / width)
    return o * lax.rsqrt(ms + EPS)


def _mod_kernel(cp_ref, cs_ref, w_ref, b_ref, op_ref, os_ref):
    w = w_ref[...].astype(BF16)
    b = b_ref[...]
    op_ref[...] = _dot(_silu(cp_ref[...]).astype(BF16), w) + b
    os_ref[...] = _dot(_silu(cs_ref[...]).astype(BF16), w) + b


def _mod_call(cp8, cs, w_ada, b_ada):
    tn = 1024
    n = 6 * D
    return pl.pallas_call(
        _mod_kernel,
        grid=(DEPTH, n // tn),
        in_specs=[
            pl.BlockSpec((SUBLANES, D), lambda l, j: (0, 0)),
            pl.BlockSpec((BS, D), lambda l, j: (0, 0)),
            pl.BlockSpec((None, D, tn), lambda l, j: (l, 0, j)),
            pl.BlockSpec((None, 1, tn), lambda l, j: (l, 0, j)),
        ],
        out_specs=[
            pl.BlockSpec((None, SUBLANES, tn), lambda l, j: (l, 0, j)),
            pl.BlockSpec((None, BS, tn), lambda l, j: (l, 0, j)),
        ],
        out_shape=[
            jax.ShapeDtypeStruct((DEPTH, SUBLANES, n), F32),
            jax.ShapeDtypeStruct((DEPTH, BS, n), F32),
        ],
        compiler_params=_cparams(("arbitrary", "arbitrary")),
        name="adaln_mod",
    )(cp8, cs, w_ada, b_ada)


ROW_CHUNK = 128


def _fill_hn(x_ref, g_ref, sc_ref, sh_ref, hn_ref, *, tile, tm, prompt, seq_tiles):
    g = g_ref[...]
    if prompt:
        b = tile // seq_tiles
        sc = sc_ref[pl.ds(b, 1), :]
        sh = sh_ref[pl.ds(b, 1), :]

    def body(r, carry):
        rows = pl.ds(pl.multiple_of(r * ROW_CHUNK, ROW_CHUNK), ROW_CHUNK)
        if prompt:
            y = _norm_mod(x_ref[rows, :], g, sc, sh)
        else:
            y = _norm_mod(x_ref[rows, :], g, sc_ref[...], sh_ref[...])
        hn_ref[rows, :] = y.astype(BF16)
        return carry

    lax.fori_loop(0, tm // ROW_CHUNK, body, 0)


def _mod_specs(layer, chunks, prompt, width, col_of):
    rows = SUBLANES if prompt else BS
    per = D // width
    return [pl.BlockSpec((None, rows, width), (lambda i, j, c=c: (layer, 0, c * per + col_of(i, j)))) for c in chunks]


def _proj_kernel(x_ref, g_ref, sc_ref, sh_ref, w_ref, o_ref, hn_ref, *, tm, prompt, seq_tiles):
    @pl.when(pl.program_id(1) == 0)
    def _():
        _fill_hn(x_ref, g_ref, sc_ref, sh_ref, hn_ref, tile=pl.program_id(0), tm=tm, prompt=prompt,
                 seq_tiles=seq_tiles)

    o_ref[...] = _dot_nt(hn_ref[...], w_ref[...])


def _proj_call(x, norm_g, mod, w, *, layer, sc_chunk, sh_chunk, prompt, tm, tn):
    m = x.shape[0]
    n = w.shape[-2]
    kern = functools.partial(_proj_kernel, tm=tm, prompt=prompt, seq_tiles=TP // tm if prompt else 1)
    sc_spec, sh_spec = _mod_specs(layer, (sc_chunk, sh_chunk), prompt, D, lambda i, j: 0)
    return pl.pallas_call(
        kern,
        grid=(m // tm, n // tn),
        in_specs=[
            pl.BlockSpec((tm, D), lambda i, j: (i, 0)),
            pl.BlockSpec((None, 1, D), lambda i, j: (layer, 0, 0)),
            sc_spec,
            sh_spec,
            pl.BlockSpec((None, tn, D), lambda i, j: (layer, j, 0)),
        ],
        out_specs=pl.BlockSpec((tm, tn), lambda i, j: (i, j)),
        out_shape=jax.ShapeDtypeStruct((m, n), F32),
        scratch_shapes=[pltpu.VMEM((tm, D), BF16)],
        compiler_params=_cparams(("arbitrary", "arbitrary")),
        name="in_proj_p" if prompt else "in_proj_s",
    )(x, norm_g, mod, mod, w)


def _resid_kernel(a_ref, w_ref, x_ref, g_ref, o_ref, *, tm, prompt, seq_tiles):
    acc = _dot(a_ref[...].astype(BF16), w_ref[...])
    if prompt:
        b = pl.program_id(0) // seq_tiles
        o_ref[...] = x_ref[...] + g_ref[pl.ds(b, 1), :] * acc
    else:
        g = g_ref[...]
        for t in range(tm // BS):
            rows = slice(t * BS, (t + 1) * BS)
            o_ref[rows, :] = x_ref[rows, :] + g * acc[rows, :]


def _resid_call(a, w, x, mod, *, layer, g_chunk, prompt, tm, tn, name):
    m, k = a.shape
    kern = functools.partial(_resid_kernel, tm=tm, prompt=prompt, seq_tiles=TP // tm if prompt else 1)
    (g_spec,) = _mod_specs(layer, (g_chunk,), prompt, tn, lambda i, j: j)
    return pl.pallas_call(
        kern,
        grid=(m // tm, D // tn),
        in_specs=[
            pl.BlockSpec((tm, k), lambda i, j: (i, 0)),
            pl.BlockSpec((None, k, tn), lambda i, j: (layer, 0, j)),
            pl.BlockSpec((tm, tn), lambda i, j: (i, j)),
            g_spec,
        ],
        out_specs=pl.BlockSpec((tm, tn), lambda i, j: (i, j)),
        out_shape=jax.ShapeDtypeStruct((m, D), F32),
        compiler_params=_cparams(("arbitrary", "arbitrary")),
        name=name,
    )(a, w, x, mod)


def _causal_conv_rows(x_ref, cols, cw, cb, *, n_rows, pre):
    taps = cw.shape[0]
    first_row = lax.broadcasted_iota(jnp.int32, (SUBLANES, 1), 0) == 0
    w = [cw[k:k + 1, :] for k in range(taps)]
    x = x_ref[pre - SUBLANES:pre, cols]
    acc = w[0] * x
    rots = []
    for l in range(1, taps - 1):
        rots.append(pltpu.roll(acc, 1, axis=0))
        acc = w[l] * x + rots[-1]
    rots.append(pltpu.roll(acc, 1, axis=0))
    for g in range(n_rows // SUBLANES):
        x = x_ref[pre + g * SUBLANES:pre + (g + 1) * SUBLANES, cols]
        acc = w[0] * x
        new_rots = []
        for l in range(1, taps):
            new_rots.append(pltpu.roll(acc, 1, axis=0))
            shifted = jnp.where(first_row, rots[l - 1], new_rots[-1])
            acc = (w[l] * x + shifted) if l < taps - 1 else (cb + w[l] * x) + shifted
        yield g, acc
        rots = new_rots


def _ffn_up_kernel(*refs, tm, prompt, seq_tiles, pre, shift):
    (x_ref, g_ref, sc_ref, sh_ref, wg_ref, wv_ref, cwg_ref, cwv_ref, cbg_ref, cbv_ref) = refs[:10]
    if prompt:
        act_ref, stg_ref, stv_ref, hn_ref, ue_ref, carry_ref = refs[10:]
    else:
        s0g_ref, s1g_ref, s0v_ref, s1v_ref, act_ref, stg_ref, stv_ref, hn_ref, ue_ref = refs[10:]
    i = pl.program_id(0)
    j = pl.program_id(1)
    gate_cols = slice(0, FF_TN)
    val_cols = slice(FF_TN, 2 * FF_TN)

    @pl.when(j == 0)
    def _():
        _fill_hn(x_ref, g_ref, sc_ref, sh_ref, hn_ref, tile=i, tm=tm, prompt=prompt, seq_tiles=seq_tiles)

    if prompt:
        first = (i % seq_tiles) == 0

        @pl.when(first)
        def _():
            ue_ref[0:pre, :] = jnp.zeros((pre, 2 * FF_TN), F32)

        @pl.when(jnp.logical_not(first))
        def _():
            ue_ref[0:pre, :] = carry_ref[j]
    else:
        ue_ref[0:BS, gate_cols] = s0g_ref[...]
        ue_ref[0:BS, val_cols] = s0v_ref[...]
        ue_ref[BS:2 * BS, gate_cols] = s1g_ref[...]
        ue_ref[BS:2 * BS, val_cols] = s1v_ref[...]

    hn = hn_ref[...]
    ue_ref[pre:pre + tm, gate_cols] = _dot(hn, wg_ref[...])
    ue_ref[pre:pre + tm, val_cols] = _dot(hn, wv_ref[...])

    cwg, cwv, cbg, cbv = cwg_ref[...], cwv_ref[...], cbg_ref[...], cbv_ref[...]
    if prompt:
        gates = _causal_conv_rows(ue_ref, gate_cols, cwg, cbg, n_rows=tm, pre=pre)
        vals = _causal_conv_rows(ue_ref, val_cols, cwv, cbv, n_rows=tm, pre=pre)
        pending = []
        for (g, gate), (_, val) in zip(gates, vals):
            pending.append(_silu(gate) * val)
            if len(pending) == 2:
                act_ref[(g - 1) * SUBLANES:(g + 1) * SUBLANES, :] = jnp.concatenate(pending, axis=0).astype(BF16)
                pending = []
        carry_ref[j] = ue_ref[tm:tm + pre, :]
        stg_ref[...] = ue_ref[pre + tm - (FFN_K - 1):pre + tm, gate_cols]
        stv_ref[...] = ue_ref[pre + tm - (FFN_K - 1):pre + tm, val_cols]
    else:
        def conv(base, cols, cw, cb):
            u = cb + cw[0:1, :] * ue_ref[base + pre - 2 * shift:base + pre - 2 * shift + rc, cols]
            u = u + cw[1:2, :] * ue_ref[base + pre - shift:base + pre - shift + rc, cols]
            return u + cw[2:3, :] * ue_ref[base + pre:base + pre + rc, cols]

        rc = 16
        for r in range(tm // rc):
            gate = conv(r * rc, gate_cols, cwg, cbg)
            val = conv(r * rc, val_cols, cwv, cbv)
            act_ref[r * rc:(r + 1) * rc, :] = (_silu(gate) * val).astype(BF16)
        for k in range(FFN_K - 1):
            rows = slice(pre + tm - (FFN_K - 1 - k) * BS, pre + tm - (FFN_K - 2 - k) * BS)
            stg_ref[k] = ue_ref[rows, gate_cols]
            stv_ref[k] = ue_ref[rows, val_cols]


def _ffn_up_call(x, norm_g, mod, w, cw, cb, st_in, *, layer, prompt, tm):
    m = x.shape[0]
    seq_tiles = TP // tm if prompt else 1
    pre = SUBLANES if prompt else (FFN_K - 1) * BS
    shift = 1 if prompt else BS
    kern = functools.partial(_ffn_up_kernel, tm=tm, prompt=prompt, seq_tiles=seq_tiles, pre=pre, shift=shift)
    mod_rows = SUBLANES if prompt else BS
    in_specs = [
        pl.BlockSpec((tm, D), lambda i, j: (i, 0)),
        pl.BlockSpec((None, 1, D), lambda i, j: (layer, 0, 0)),
        pl.BlockSpec((None, mod_rows, D), lambda i, j: (layer, 0, 4)),
        pl.BlockSpec((None, mod_rows, D), lambda i, j: (layer, 0, 3)),
        pl.BlockSpec((None, D, FF_TN), lambda i, j: (layer, 0, j)),
        pl.BlockSpec((None, D, FF_TN), lambda i, j: (layer, 0, FF_NJ + j)),
        pl.BlockSpec((None, FFN_K, FF_TN), lambda i, j: (layer, 0, j)),
        pl.BlockSpec((None, FFN_K, FF_TN), lambda i, j: (layer, 0, FF_NJ + j)),
        pl.BlockSpec((None, 1, FF_TN), lambda i, j: (layer, 0, j)),
        pl.BlockSpec((None, 1, FF_TN), lambda i, j: (layer, 0, FF_NJ + j)),
    ]
    args = [x, norm_g, mod, mod, w, w, cw, cw, cb, cb]
    scratch = [pltpu.VMEM((tm, D), BF16), pltpu.VMEM((pre + tm, 2 * FF_TN), F32)]
    if prompt:
        st_spec = pl.BlockSpec((None, FFN_K - 1, FF_TN), lambda i, j: (i, 0, j))
        st_shape = jax.ShapeDtypeStruct((m // tm, FFN_K - 1, D_FF), F32)
        scratch.append(pltpu.VMEM((FF_NJ, pre, 2 * FF_TN), F32))
    else:
        for half in range(2):
            in_specs += [pl.BlockSpec((None, None, BS, FF_TN),
                                      lambda i, j, k=k, half=half: (layer, k, 0, half * FF_NJ + j))
                         for k in range(FFN_K - 1)]
            args += [st_in] * (FFN_K - 1)
        st_spec = pl.BlockSpec((FFN_K - 1, BS, FF_TN), lambda i, j: (0, 0, j))
        st_shape = jax.ShapeDtypeStruct((FFN_K - 1, BS, D_FF), F32)
    return pl.pallas_call(
        kern,
        grid=(m // tm, FF_NJ),
        in_specs=in_specs,
        out_specs=[pl.BlockSpec((tm, FF_TN), lambda i, j: (i, j)), st_spec, st_spec],
        out_shape=[jax.ShapeDtypeStruct((m, D_FF), BF16), st_shape, st_shape],
        scratch_shapes=scratch,
        compiler_params=_cparams(("arbitrary", "arbitrary")),
        name="ffn_up_p" if prompt else "ffn_up_s",
    )(*args)


def _final_kernel(x_ref, g_ref, o_ref):
    x = x_ref[...]
    ms = jnp.mean(x * x, axis=-1, keepdims=True)
    o_ref[...] = (x * lax.rsqrt(ms + EPS)) * g_ref[...]


def _final_call(x, g):
    m = x.shape[0]
    tm = 256
    return pl.pallas_call(
        _final_kernel,
        grid=(m // tm,),
        in_specs=[pl.BlockSpec((tm, D), lambda i: (i, 0)), pl.BlockSpec((1, D), lambda i: (0, 0))],
        out_specs=pl.BlockSpec((tm, D), lambda i: (i, 0)),
        out_shape=jax.ShapeDtypeStruct((m, D), F32),
        compiler_params=_cparams(("arbitrary",)),
        name="final_norm",
    )(x, g)


def _split3(x):
    x1 = x.astype(BF16)
    r = x - x1.astype(F32)
    x2 = r.astype(BF16)
    r = r - x2.astype(F32)
    return x1, x2, r.astype(BF16)


def _ret_log_gamma(h):
    return math.log1p(-(2.0 ** (-5.0 - h)))


def _lru_gates(xc, wa_ref, ba_ref, wx_ref, bx_ref, lam_ref):
    xb = xc.astype(BF16)
    r = _sigmoid(_dot(xb, wa_ref[...]) + ba_ref[...])
    i = _sigmoid(_dot(xb, wx_ref[...]) + bx_ref[...])
    log_a = (-LRU_C * r) * _softplus(-lam_ref[...])
    a = jnp.exp(log_a)
    u = jnp.sqrt(-jnp.tanh(log_a) * (a * a + 1.0)) * (i * xc)
    return a, u


MIX_TC = 256


def _mixer_prompt_kernel(z_ref, cos_ref, sin_ref, wal_ref, bal_ref, gng_ref, lcw_ref, lcb_ref,
                         wa_ref, ba_ref, wx_ref, bx_ref, lam_ref, rng_ref,
                         mix_ref, sg_ref, sr_ref, hl_ref, lc_ref,
                         stg_ref, str_ref, h_ref, lxe_ref, a_ref, u_ref, hs_ref, att_s, rdec_ref):
    t_id = pl.program_id(1)
    nt = pl.num_programs(1)
    tc = MIX_TC
    pre = SUBLANES

    @pl.when(t_id == 0)
    def _():
        stg_ref[...] = jnp.zeros_like(stg_ref)
        str_ref[...] = jnp.zeros_like(str_ref)
        h_ref[...] = jnp.zeros_like(h_ref)
        lxe_ref[0:pre, :] = jnp.zeros((pre, W_LRU), F32)

    lxe_ref[pre:pre + tc, :] = z_ref[:, LX0:LX0 + W_LRU]

    row_c = lax.broadcasted_iota(jnp.int32, (CHUNK, CHUNK), 0)
    col_c = lax.broadcasted_iota(jnp.int32, (CHUNK, CHUNK), 1)
    tri = (col_c <= row_c).astype(BF16)
    rel = (row_c - col_c).astype(F32)
    rowf = lax.broadcasted_iota(jnp.int32, (CHUNK, LANES), 0).astype(F32)
    row1 = lax.broadcasted_iota(jnp.int32, (CHUNK, 1), 0)
    sub_row = lax.broadcasted_iota(jnp.int32, (SUBLANES, 1), 0)
    lane_id = lax.broadcasted_iota(jnp.int32, (SUBLANES, LANES), 1)

    @pl.when(jnp.logical_and(pl.program_id(0) == 0, t_id == 0))
    def _():
        for h in range(H_RET):
            lg = _ret_log_gamma(h)
            rdec_ref[h, 0, :, 0:CHUNK] = jnp.where(rel >= 0, jnp.exp(jnp.maximum(rel, 0.0) * lg), 0.0)
            rdec_ref[h, 1] = jnp.exp((rowf + 1.0) * lg)
            rdec_ref[h, 2] = jnp.exp((CHUNK - 1.0 - rowf) * lg)

    def chunk_body(c, carry):
        r0 = pl.multiple_of(c * CHUNK, CHUNK)
        rows = pl.ds(r0, CHUNK)

        zq = z_ref[rows, Q0:Q0 + H_GLA * DKP] * (DK_GLA ** -0.5)
        zk = z_ref[rows, K0:K0 + H_GLA * DKP]
        lr = z_ref[rows, LR0:LR0 + LANES].astype(BF16)
        la = _log_sigmoid(_dot(lr, wal_ref[...]) + bal_ref[...]) * (1.0 / GLA_TAU)
        p1, p2, p3 = _split3(la)
        bcs = _dot(tri, p1) + _dot(tri, p2) + _dot(tri, p3)
        b_last = bcs[CHUNK - 1:CHUNK, :]
        qb = (zq * jnp.exp(bcs)).astype(BF16)
        kb = (zk * jnp.exp(b_last - bcs)).astype(BF16)
        e_last = jnp.exp(b_last)

        bcs2 = bcs * LOG2E
        for i in range(CHUNK // SUB):
            blk = slice(i * SUB, (i + 1) * SUB)
            q_blk = zq[blk, :]
            b_blk = bcs[blk, :]
            b2_blk = bcs2[blk, :]
            if i > 0:
                r_i = bcs[i * SUB - 1:i * SUB, :]
                q_i = (q_blk * jnp.exp(b_blk - r_i)).astype(BF16)
                kk = (zk * jnp.exp(jnp.where(row1 < i * SUB, r_i - bcs, NEG))).astype(BF16)

            att = [[jnp.zeros((SUBLANES, LANES), F32) for _ in range(SUB // SUBLANES)] for _ in range(H_GLA)]
            for sl in range(SUB):
                s = i * SUB + sl
                for g in range(sl // SUBLANES, SUB // SUBLANES):
                    rows_g = slice(g * SUBLANES, (g + 1) * SUBLANES)
                    diff = b2_blk[rows_g, :] - bcs2[s:s + 1, :]
                    if sl > g * SUBLANES:
                        diff = jnp.where(sub_row >= sl - g * SUBLANES, diff, NEG)
                    w = q_blk[rows_g, :] * zk[s:s + 1, :] * jnp.exp2(diff)
                    for h in range(H_GLA):
                        col = jnp.sum(w[:, h * DKP:(h + 1) * DKP], axis=-1, keepdims=True)
                        att[h][g] = jnp.where(lane_id == s, col, att[h][g])
            for h in range(H_GLA):
                a_h = jnp.concatenate(att[h], axis=0)[:, :CHUNK]
                if i > 0:
                    a_h = a_h + _dot_nt(q_i[:, h * DKP:(h + 1) * DKP], kk[:, h * DKP:(h + 1) * DKP])
                att_s[h, blk, 0:CHUNK] = a_h

        for h in range(H_GLA):
            v_h = z_ref[rows, V0 + h * DVP:V0 + (h + 1) * DVP].astype(BF16)
            st_old = stg_ref[h]
            o_h = _dot(att_s[h, :, 0:CHUNK].astype(BF16), v_h)
            o_h = o_h + _dot_nt(qb[:, h * DKP:(h + 1) * DKP], st_old.astype(BF16))
            stg_ref[h] = e_last[:, h * DKP:(h + 1) * DKP] * st_old + _dot_tn(v_h, kb[:, h * DKP:(h + 1) * DKP])
            gate = z_ref[rows, GG0 + h * DVP:GG0 + (h + 1) * DVP]
            y = _head_norm(o_h, DV_GLA) * gng_ref[:, h * DVP:(h + 1) * DVP] * _silu(gate)
            mix_ref[rows, MG0 + h * DVP:MG0 + (h + 1) * DVP] = y.astype(BF16)

        cosv = cos_ref[rows, :]
        sinv = sin_ref[rows, :]
        for h in range(H_RET):
            lg = _ret_log_gamma(h)
            cols = slice(h * DK_RET, (h + 1) * DK_RET)
            xq = z_ref[rows, RQ0 + h * DK_RET:RQ0 + (h + 1) * DK_RET]
            xk = z_ref[rows, RK0 + h * DK_RET:RK0 + (h + 1) * DK_RET]
            q = (xq * cosv + pltpu.roll(xq, DK_RET // 2, axis=1) * sinv) * (DK_RET ** -0.5)
            k = xk * cosv + pltpu.roll(xk, DK_RET // 2, axis=1) * sinv
            v = z_ref[rows, RV0 + h * DK_RET:RV0 + (h + 1) * DK_RET].astype(BF16)
            decay = rdec_ref[h, 0, :, 0:CHUNK]
            q_dec = rdec_ref[h, 1]
            k_dec = rdec_ref[h, 2]
            c_dec = math.exp(CHUNK * lg)
            qb16 = q.astype(BF16)
            att_r = _dot_nt(qb16, k.astype(BF16)) * decay
            s_old = str_ref[h]
            o = _dot(att_r.astype(BF16), v) + _dot(qb16, s_old.astype(BF16)) * q_dec
            str_ref[h] = c_dec * s_old + _dot_tn((k * k_dec).astype(BF16), v)
            gate = z_ref[rows, RG0 + h * DK_RET:RG0 + (h + 1) * DK_RET]
            y = _head_norm(o, DK_RET) * rng_ref[:, cols] * _silu(gate)
            mix_ref[rows, MR0 + h * DK_RET:MR0 + (h + 1) * DK_RET] = y.astype(BF16)

        return carry

    lax.fori_loop(0, tc // CHUNK, chunk_body, 0, unroll=4)

    for g, y in _causal_conv_rows(lxe_ref, slice(0, W_LRU), lcw_ref[...], lcb_ref[...], n_rows=tc, pre=pre):
        hs_ref[g * SUBLANES:(g + 1) * SUBLANES, :] = y
    a, u = _lru_gates(hs_ref[...], wa_ref, ba_ref, wx_ref, bx_ref, lam_ref)
    a_ref[...] = a
    u_ref[...] = u

    def scan_body(t, h):
        h = a_ref[pl.ds(t, 1), :] * h + u_ref[pl.ds(t, 1), :]
        hs_ref[pl.ds(t, 1), :] = h
        return h

    h_fin = lax.fori_loop(0, tc, scan_body, h_ref[0:1, :], unroll=8)
    h_ref[0:1, :] = h_fin

    def out_body(c, carry):
        rows = pl.ds(pl.multiple_of(c * CHUNK, CHUNK), CHUNK)
        y = hs_ref[rows, :] * _gelu_tanh(z_ref[rows, LG0:LG0 + W_LRU])
        mix_ref[rows, ML0:ML0 + W_LRU] = y.astype(BF16)
        return carry

    lax.fori_loop(0, tc // CHUNK, out_body, 0)

    lxe_ref[0:pre, :] = lxe_ref[tc:tc + pre, :]

    @pl.when(t_id == nt - 1)
    def _():
        for h in range(H_GLA):
            sg_ref[h] = stg_ref[h].T[:DK_GLA, :DV_GLA]
        sr_ref[...] = str_ref[...]
        hl_ref[...] = h_fin
        lc_ref[...] = lxe_ref[tc + pre - (LRU_K - 1):tc + pre, :]


def _mixer_prompt_call(z, cos_t, sin_t, pw, *, layer):
    tc = MIX_TC
    nt = TP // tc

    def lspec(shape):
        return pl.BlockSpec((None,) + shape, lambda b, t: (layer,) + (0,) * len(shape))

    in_specs = [
        pl.BlockSpec((tc, NZ), lambda b, t: (b * nt + t, 0)),
        pl.BlockSpec((tc, LANES), lambda b, t: (t, 0)),
        pl.BlockSpec((tc, LANES), lambda b, t: (t, 0)),
        lspec((LANES, H_GLA * DKP)), lspec((1, H_GLA * DKP)), lspec((1, H_GLA * DVP)),
        lspec((LRU_K, W_LRU)), lspec((1, W_LRU)),
        lspec((W_LRU, W_LRU)), lspec((1, W_LRU)), lspec((W_LRU, W_LRU)), lspec((1, W_LRU)), lspec((1, W_LRU)),
        lspec((1, H_RET * DK_RET)),
    ]
    out_specs = [
        pl.BlockSpec((tc, NMIX), lambda b, t: (b * nt + t, 0)),
        pl.BlockSpec((None, H_GLA, DK_GLA, DV_GLA), lambda b, t: (b, 0, 0, 0)),
        pl.BlockSpec((None, H_RET, DK_RET, DK_RET), lambda b, t: (b, 0, 0, 0)),
        pl.BlockSpec((None, 1, W_LRU), lambda b, t: (b, 0, 0)),
        pl.BlockSpec((None, LRU_K - 1, W_LRU), lambda b, t: (b, 0, 0)),
    ]
    out_shape = [
        jax.ShapeDtypeStruct((BP * TP, NMIX), BF16),
        jax.ShapeDtypeStruct((BP, H_GLA, DK_GLA, DV_GLA), F32),
        jax.ShapeDtypeStruct((BP, H_RET, DK_RET, DK_RET), F32),
        jax.ShapeDtypeStruct((BP, 1, W_LRU), F32),
        jax.ShapeDtypeStruct((BP, LRU_K - 1, W_LRU), F32),
    ]
    scratch = [
        pltpu.VMEM((H_GLA, DVP, DKP), F32),
        pltpu.VMEM((H_RET, DK_RET, DK_RET), F32),
        pltpu.VMEM((SUBLANES, W_LRU), F32),
        pltpu.VMEM((SUBLANES + tc, W_LRU), F32),
        pltpu.VMEM((tc, W_LRU), F32), pltpu.VMEM((tc, W_LRU), F32), pltpu.VMEM((tc, W_LRU), F32),
        pltpu.VMEM((H_GLA, CHUNK, LANES), F32),
        pltpu.VMEM((H_RET, 3, CHUNK, LANES), F32),
    ]
    return pl.pallas_call(
        _mixer_prompt_kernel,
        grid=(BP, nt),
        in_specs=in_specs,
        out_specs=out_specs,
        out_shape=out_shape,
        scratch_shapes=scratch,
        compiler_params=_cparams(("arbitrary", "arbitrary")),
        name="mixer_p",
    )(z, cos_t, sin_t, pw["w_alpha"], pw["b_alpha"], pw["gla_norm_g"], pw["lru_conv_w"], pw["lru_conv_b"],
      pw["lru_wa"], pw["lru_ba"], pw["lru_wx"], pw["lru_bx"], pw["lru_lam"], pw["ret_norm_g"])


MIX_BB = 8


def _mixer_sample_kernel(z_ref, cos_ref, sin_ref, wal_ref, bal_ref, gng_ref, lcw_ref, lcb_ref,
                         wa_ref, ba_ref, wx_ref, bx_ref, lam_ref, rng_ref,
                         sg0_ref, sr0_ref, h0_ref, lc0_ref,
                         mix_ref, sg_ref, sr_ref, hl_ref, lc_ref,
                         qb_s, kb_s, v_s, og_s, rq_s, rk_s, rv_s, or_s, sp_ref, et_ref):
    bb = MIX_BB
    nrow = TS * bb

    @pl.when(pl.program_id(0) == 0)
    def _():
        sp_ref[...] = jnp.zeros_like(sp_ref)
        et_ref[...] = jnp.zeros_like(et_ref)

    def zcols(c0, width):
        return z_ref[:, :, c0:c0 + width].reshape(nrow, width)

    def slab(x, t):
        return x[t * bb:(t + 1) * bb, :]

    zq = zcols(Q0, H_GLA * DKP) * (DK_GLA ** -0.5)
    zk = zcols(K0, H_GLA * DKP)
    zv = zcols(V0, H_GLA * DVP)
    la = _log_sigmoid(_dot(zcols(LR0, LANES).astype(BF16), wal_ref[...]) + bal_ref[...]) * (1.0 / GLA_TAU)
    bs = [slab(la, 0)]
    for t in range(1, TS):
        bs.append(bs[-1] + slab(la, t))
    b_last = bs[-1]
    e_last = jnp.exp(b_last)
    o_t = []
    for t in range(TS):
        acc = None
        for s in range(t + 1):
            w = slab(zq, t) * slab(zk, s) * jnp.exp(bs[t] - bs[s])
            parts = []
            for h in range(H_GLA):
                a_ts = jnp.sum(w[:, h * DKP:(h + 1) * DKP], axis=-1, keepdims=True)
                parts.append(a_ts * slab(zv, s)[:, h * DVP:(h + 1) * DVP])
            contrib = jnp.concatenate(parts, axis=1)
            acc = contrib if acc is None else acc + contrib
        o_t.append(acc)
    o_intra = jnp.concatenate(o_t, axis=0)
    b_all = jnp.concatenate(bs, axis=0)
    qb = zq * jnp.exp(b_all)
    kb = zk * jnp.exp(jnp.concatenate([b_last] * TS, axis=0) - b_all)
    for h in range(H_GLA):
        qb_s[h] = qb[:, h * DKP:(h + 1) * DKP]
        kb_s[h] = kb[:, h * DKP:(h + 1) * DKP]
        for p in range(DVP // LANES):
            v_s[h * (DVP // LANES) + p] = zv[:, h * DVP + p * LANES:h * DVP + (p + 1) * LANES]
        et_ref[h, 0:bb, :] = e_last[:, h * DKP:(h + 1) * DKP]
    e_cols = [et_ref[h].T for h in range(H_GLA)]

    for b in range(bb):
        seq = pl.ds(b, TS, stride=bb)
        for h in range(H_GLA):
            s0 = sg0_ref[b, h]
            sp_ref[0:DK_GLA, 0:DV_GLA] = s0
            q_bh = qb_s[h, seq, :]
            k_bh = kb_s[h, seq, :]
            v_bh = jnp.concatenate([v_s[h * (DVP // LANES) + p, seq, :] for p in range(DVP // LANES)], axis=1)
            o_bh = _dot(q_bh, sp_ref[...])
            for p in range(DVP // LANES):
                og_s[h * (DVP // LANES) + p, seq, :] = o_bh[:, p * LANES:(p + 1) * LANES]
            upd = _dot_tn(k_bh, v_bh)
            s_new = e_cols[h][:, b:b + 1] * sp_ref[...] + upd
            sg_ref[b, h] = s_new[:DK_GLA, :DV_GLA]

    og = jnp.concatenate([og_s[i] for i in range(H_GLA * DVP // LANES)], axis=1) + o_intra
    for h in range(H_GLA):
        cols = slice(h * DVP, (h + 1) * DVP)
        y = _head_norm(og[:, cols], DV_GLA) * gng_ref[:, cols] * _silu(zcols(GG0 + h * DVP, DVP))
        mix_ref[:, :, MG0 + h * DVP:MG0 + (h + 1) * DVP] = y.reshape(TS, bb, DVP)

    cosv = jnp.concatenate([jnp.broadcast_to(cos_ref[t:t + 1, :], (bb, LANES)) for t in range(TS)], axis=0)
    sinv = jnp.concatenate([jnp.broadcast_to(sin_ref[t:t + 1, :], (bb, LANES)) for t in range(TS)], axis=0)
    trow = lax.broadcasted_iota(jnp.int32, (TS, LANES), 0).astype(F32)
    r_intra = []
    for h in range(H_RET):
        lg = _ret_log_gamma(h)
        xq = zcols(RQ0 + h * DK_RET, DK_RET)
        xk = zcols(RK0 + h * DK_RET, DK_RET)
        q = (xq * cosv + pltpu.roll(xq, DK_RET // 2, axis=1) * sinv) * (DK_RET ** -0.5)
        k = xk * cosv + pltpu.roll(xk, DK_RET // 2, axis=1) * sinv
        v = zcols(RV0 + h * DK_RET, DK_RET)
        rq_s[h] = q
        rk_s[h] = k
        rv_s[h] = v
        outs = []
        for t in range(TS):
            acc = None
            for s in range(t + 1):
                a_ts = jnp.sum(slab(q, t) * slab(k, s), axis=-1, keepdims=True) * math.exp((t - s) * lg)
                contrib = a_ts * slab(v, s)
                acc = contrib if acc is None else acc + contrib
            outs.append(acc)
        r_intra.append(jnp.concatenate(outs, axis=0))

    for b in range(bb):
        seq = pl.ds(b, TS, stride=bb)
        for h in range(H_RET):
            lg = _ret_log_gamma(h)
            q_dec = jnp.exp((trow + 1.0) * lg)
            k_dec = jnp.exp((TS - 1.0 - trow) * lg)
            s0 = sr0_ref[b, h]
            q_bh = rq_s[h, seq, :]
            k_bh = rk_s[h, seq, :]
            v_bh = rv_s[h, seq, :]
            or_s[h, seq, :] = _dot(q_bh, s0) * q_dec
            sr_ref[b, h] = math.exp(TS * lg) * s0 + _dot_tn(k_bh * k_dec, v_bh)

    for h in range(H_RET):
        cols = slice(h * DK_RET, (h + 1) * DK_RET)
        o = or_s[h] + r_intra[h]
        y = _head_norm(o, DK_RET) * rng_ref[:, cols] * _silu(zcols(RG0 + h * DK_RET, DK_RET))
        mix_ref[:, :, MR0 + h * DK_RET:MR0 + (h + 1) * DK_RET] = y.reshape(TS, bb, DK_RET)

    lx = zcols(LX0, W_LRU)
    xe = [lc0_ref[:, k * W_LRU:(k + 1) * W_LRU] for k in range(LRU_K - 1)] + [slab(lx, t) for t in range(TS)]
    cw = lcw_ref[...]
    xc = []
    for t in range(TS):
        acc = lcb_ref[...] + cw[0:1, :] * xe[t]
        for k in range(1, LRU_K):
            acc = acc + cw[k:k + 1, :] * xe[t + k]
        xc.append(acc)
    a, u = _lru_gates(jnp.concatenate(xc, axis=0), wa_ref, ba_ref, wx_ref, bx_ref, lam_ref)
    h = h0_ref[...]
    hs = []
    for t in range(TS):
        h = slab(a, t) * h + slab(u, t)
        hs.append(h)
    y = jnp.concatenate(hs, axis=0) * _gelu_tanh(zcols(LG0, W_LRU))
    mix_ref[:, :, ML0:ML0 + W_LRU] = y.reshape(TS, bb, W_LRU)
    hl_ref[...] = h
    for k in range(LRU_K - 1):
        lc_ref[:, k * W_LRU:(k + 1) * W_LRU] = xe[TS + k]


def _mixer_sample_call(z, cos_t, sin_t, pw, st_gla, st_ret, st_lru, st_lconv, *, layer):
    bb = MIX_BB

    def lspec(shape):
        return pl.BlockSpec((None,) + shape, lambda j: (layer,) + (0,) * len(shape))

    in_specs = [
        pl.BlockSpec((TS, bb, NZ), lambda j: (0, j, 0)),
        pl.BlockSpec((TS, LANES), lambda j: (0, 0)),
        pl.BlockSpec((TS, LANES), lambda j: (0, 0)),
        lspec((LANES, H_GLA * DKP)), lspec((1, H_GLA * DKP)), lspec((1, H_GLA * DVP)),
        lspec((LRU_K, W_LRU)), lspec((1, W_LRU)),
        lspec((W_LRU, W_LRU)), lspec((1, W_LRU)), lspec((W_LRU, W_LRU)), lspec((1, W_LRU)), lspec((1, W_LRU)),
        lspec((1, H_RET * DK_RET)),
        pl.BlockSpec((None, bb, H_GLA, DK_GLA, DV_GLA), lambda j: (layer, j, 0, 0, 0)),
        pl.BlockSpec((None, bb, H_RET, DK_RET, DK_RET), lambda j: (layer, j, 0, 0, 0)),
        pl.BlockSpec((None, bb, W_LRU), lambda j: (layer, j, 0)),
        pl.BlockSpec((None, bb, (LRU_K - 1) * W_LRU), lambda j: (layer, j, 0)),
    ]
    out_specs = [
        pl.BlockSpec((TS, bb, NMIX), lambda j: (0, j, 0)),
        pl.BlockSpec((bb, H_GLA, DK_GLA, DV_GLA), lambda j: (j, 0, 0, 0)),
        pl.BlockSpec((bb, H_RET, DK_RET, DK_RET), lambda j: (j, 0, 0, 0)),
        pl.BlockSpec((bb, W_LRU), lambda j: (j, 0)),
        pl.BlockSpec((bb, (LRU_K - 1) * W_LRU), lambda j: (j, 0)),
    ]
    out_shape = [
        jax.ShapeDtypeStruct((TS, BS, NMIX), F32),
        jax.ShapeDtypeStruct((BS, H_GLA, DK_GLA, DV_GLA), F32),
        jax.ShapeDtypeStruct((BS, H_RET, DK_RET, DK_RET), F32),
        jax.ShapeDtypeStruct((BS, W_LRU), F32),
        jax.ShapeDtypeStruct((BS, (LRU_K - 1) * W_LRU), F32),
    ]
    nrow = TS * bb
    scratch = [
        pltpu.VMEM((H_GLA, nrow, LANES), F32), pltpu.VMEM((H_GLA, nrow, LANES), F32),
        pltpu.VMEM((H_GLA * DVP // LANES, nrow, LANES), F32), pltpu.VMEM((H_GLA * DVP // LANES, nrow, LANES), F32),
        pltpu.VMEM((H_RET, nrow, LANES), F32), pltpu.VMEM((H_RET, nrow, LANES), F32),
        pltpu.VMEM((H_RET, nrow, LANES), F32), pltpu.VMEM((H_RET, nrow, LANES), F32),
        pltpu.VMEM((DKP, DVP), F32),
        pltpu.VMEM((H_GLA, LANES, LANES), F32),
    ]
    return pl.pallas_call(
        _mixer_sample_kernel,
        grid=(BS // bb,),
        in_specs=in_specs,
        out_specs=out_specs,
        out_shape=out_shape,
        scratch_shapes=scratch,
        compiler_params=_cparams(("arbitrary",)),
        name="mixer_s",
    )(z, cos_t, sin_t, pw["w_alpha"], pw["b_alpha"], pw["gla_norm_g"], pw["lru_conv_w"], pw["lru_conv_b"],
      pw["lru_wa"], pw["lru_ba"], pw["lru_wx"], pw["lru_bx"], pw["lru_lam"], pw["ret_norm_g"],
      st_gla, st_ret, st_lru, st_lconv)


def _pad_heads(w, heads, d, dp):
    lead = w.shape[:-1]
    w = w.reshape(lead + (heads, d))
    w = jnp.pad(w, [(0, 0)] * len(lead) + [(0, 0), (0, dp - d)])
    return w.reshape(lead + (heads * dp,))


def _split_cols(w, sizes):
    out, off = [], 0
    for s in sizes:
        out.append(w[..., off:off + s])
        off += s
    return out


def _rope_tables(start, length):
    half = DK_RET // 2
    freqs = ROPE_BASE ** (-jnp.arange(half, dtype=F32) / half)
    pos = start + jnp.arange(length, dtype=jnp.int32)
    ang = pos.astype(F32)[:, None] * freqs[None, :]
    cos, sin = jnp.cos(ang), jnp.sin(ang)
    return jnp.concatenate([cos, cos], axis=1), jnp.concatenate([-sin, sin], axis=1)


def _prep_weights(w_in, gla_w_alpha, gla_b_alpha, gla_norm_g, lru_conv_w, lru_conv_b, lru_w_a, lru_b_a,
                  lru_w_x, lru_b_x, lru_lambda, ret_norm_g, w_out, ffn_w_up, ffn_conv_w, ffn_conv_b, ffn_w_down):
    w_gla, w_ret = H_GLA * DV_GLA, H_RET * DK_RET
    w_rows = jnp.swapaxes(w_in, 1, 2).astype(BF16)
    pieces, off = [], 0

    def take(n_rows):
        nonlocal off
        blk = w_rows[:, off:off + n_rows]
        off += n_rows
        return blk

    def zeros(n_rows):
        return jnp.zeros((DEPTH, n_rows, D), BF16)

    for heads, d, dp in ((H_GLA, DK_GLA, DKP), (H_GLA, DK_GLA, DKP), (H_GLA, DV_GLA, DVP), (1, GLA_RANK, LANES),
                         (H_GLA, DV_GLA, DVP)):
        for _ in range(heads):
            pieces += [take(d), zeros(dp - d)]
    pieces += [take(2 * W_LRU + 4 * w_ret), zeros(NZ - NZ_USED)]
    w_in_p = jnp.concatenate(pieces, axis=1)
    w_alpha = jnp.pad(_pad_heads(gla_w_alpha, H_GLA, DK_GLA, DKP), ((0, 0), (0, LANES - GLA_RANK), (0, 0))).astype(BF16)
    eye = jnp.eye(H_LRU, dtype=F32)

    def block_diag(w):
        return jnp.einsum("lhij,hg->lhigj", w, eye).reshape(DEPTH, W_LRU, W_LRU).astype(BF16)

    wo_g, wo_l, wo_r = _split_cols(jnp.swapaxes(w_out, 1, 2), (w_gla, W_LRU, w_ret))
    w_out_p = jnp.swapaxes(jnp.concatenate([_pad_heads(wo_g, H_GLA, DV_GLA, DVP), wo_l, wo_r], axis=-1), 1, 2)
    return dict(
        w_in=w_in_p,
        w_alpha=w_alpha,
        b_alpha=_pad_heads(gla_b_alpha, H_GLA, DK_GLA, DKP)[:, None, :],
        gla_norm_g=_pad_heads(gla_norm_g, H_GLA, DV_GLA, DVP)[:, None, :],
        lru_conv_w=lru_conv_w,
        lru_conv_b=lru_conv_b[:, None, :],
        lru_wa=block_diag(lru_w_a), lru_ba=lru_b_a[:, None, :],
        lru_wx=block_diag(lru_w_x), lru_bx=lru_b_x[:, None, :],
        lru_lam=lru_lambda[:, None, :],
        ret_norm_g=ret_norm_g[:, None, :],
        w_out=w_out_p.astype(BF16),
        w_up=ffn_w_up.astype(BF16),
        ffn_cw=ffn_conv_w,
        ffn_cb=ffn_conv_b[:, None, :],
        w_down=ffn_w_down.astype(BF16),
    )


def _run_group(x, mod, pw, norm1_g, norm2_g, tables, states, *, prompt):
    tm = 1024
    tn_in, tn_out, tn_down = 1792, 1024, 512
    outs = []
    for l in range(DEPTH):
        z = _proj_call(x, norm1_g, mod, pw["w_in"], layer=l, sc_chunk=1, sh_chunk=0, prompt=prompt, tm=tm, tn=tn_in)
        if prompt:
            mixed, s_gla, s_ret, s_lru, s_lconv = _mixer_prompt_call(z, *tables, pw, layer=l)
            s_lru = s_lru.reshape(BP, W_LRU)
        else:
            st_gla, st_ret, st_lru, st_lconv, _ = states
            mixed, s_gla, s_ret, s_lru, s_lconv = _mixer_sample_call(
                z.reshape(TS, BS, NZ), *tables, pw, st_gla, st_ret, st_lru, st_lconv, layer=l)
            mixed = mixed.reshape(TS * BS, NMIX)
            s_lconv = s_lconv.reshape(BS, LRU_K - 1, W_LRU)
        x = _resid_call(mixed, pw["w_out"], x, mod, layer=l, g_chunk=2, prompt=prompt, tm=tm, tn=tn_out,
                        name="out_proj_p" if prompt else "out_proj_s")
        act, s_fg, s_fv = _ffn_up_call(x, norm2_g, mod, pw["w_up"], pw["ffn_cw"], pw["ffn_cb"],
                                       None if prompt else states[4], layer=l, prompt=prompt, tm=tm)
        x = _resid_call(act, pw["w_down"], x, mod, layer=l, g_chunk=5, prompt=prompt, tm=tm, tn=tn_down,
                        name="ffn_down_p" if prompt else "ffn_down_s")
        s_fconv = jnp.concatenate([s_fg, s_fv], axis=-1)
        if prompt:
            s_fconv = s_fconv[TP // tm - 1::TP // tm]
        else:
            s_fconv = jnp.swapaxes(s_fconv, 0, 1)
        outs.append((s_gla, s_ret, s_lru, s_lconv, s_fconv))
    return x, [jnp.stack(o) for o in zip(*outs)]


def kernel(x_prompt, x_sample, state_gla, state_ret, state_lru, state_lru_conv, state_ffn_conv, c_prompt, c_sample,
           norm1_g, norm2_g, final_g, w_ada, b_ada, w_in, gla_w_alpha, gla_b_alpha, gla_norm_g, lru_conv_w,
           lru_conv_b, lru_w_a, lru_b_a, lru_w_x, lru_b_x, lru_lambda, ret_norm_g, w_out, ffn_w_up, ffn_conv_w,
           ffn_conv_b, ffn_w_down):
    pw = _prep_weights(w_in, gla_w_alpha, gla_b_alpha, gla_norm_g, lru_conv_w, lru_conv_b, lru_w_a, lru_b_a,
                       lru_w_x, lru_b_x, lru_lambda, ret_norm_g, w_out, ffn_w_up, ffn_conv_w, ffn_conv_b, ffn_w_down)
    cp8 = jnp.pad(c_prompt, ((0, SUBLANES - BP), (0, 0)))
    mod_p, mod_s = _mod_call(cp8, c_sample, w_ada, b_ada[:, None, :])
    n1 = norm1_g[:, None, :]
    n2 = norm2_g[:, None, :]

    xp = x_prompt.reshape(BP * TP, D)
    xs = jnp.swapaxes(x_sample, 0, 1).reshape(TS * BS, D)
    st_s = (state_gla, state_ret, state_lru, state_lru_conv.reshape(DEPTH, BS, (LRU_K - 1) * W_LRU),
            jnp.swapaxes(state_ffn_conv, 1, 2))

    xp, outs_p = _run_group(xp, mod_p, pw, n1, n2, _rope_tables(0, TP), None, prompt=True)
    xs, outs_s = _run_group(xs, mod_s, pw, n1, n2, _rope_tables(PAST, TS), st_s, prompt=False)

    fg = final_g[None, :]
    y_p = _final_call(xp, fg).reshape(BP, TP, D)
    y_s = jnp.swapaxes(_final_call(xs, fg).reshape(TS, BS, D), 0, 1)
    return (y_p, y_s, *outs_p, *outs_s)
```

```python
import functools
import math

import jax
import jax.numpy as jnp
from jax import lax
from jax.experimental import pallas as pl
from jax.experimental.pallas import tpu as pltpu

F32 = jnp.float32
BF16 = jnp.bfloat16

D = 2048
DEPTH = 4
BP, TP = 4, 2048
BS, TS = 128, 8
PAST = 16384
H_GLA, DK_GLA, DV_GLA = 4, 96, 192
GLA_RANK = 16
GLA_TAU = 16.0
W_LRU, H_LRU, BLK_LRU = 768, 8, 96
LRU_K = 4
LRU_C = 8.0
H_RET, DK_RET = 4, 128
ROPE_BASE = 10000.0
D_FF = 5632
FFN_K = 3
CHUNK = 64
EPS = 1e-6

LANES = 128
SUBLANES = 8
DKP, DVP = 128, 256

Q0 = 0
K0 = Q0 + H_GLA * DKP
V0 = K0 + H_GLA * DKP
LR0 = V0 + H_GLA * DVP
GG0 = LR0 + LANES
LX0 = GG0 + H_GLA * DVP
LG0 = LX0 + W_LRU
RQ0 = LG0 + W_LRU
RK0 = RQ0 + H_RET * DK_RET
RV0 = RK0 + H_RET * DK_RET
RG0 = RV0 + H_RET * DK_RET
NZ_USED = RG0 + H_RET * DK_RET
NZ = -(-NZ_USED // 1024) * 1024
MG0 = 0
ML0 = H_GLA * DVP
MR0 = ML0 + W_LRU
NMIX = MR0 + H_RET * DK_RET

SUB = 16
NEG = -1e30
LOG2E = math.log2(math.e)
FF_TN = 512
FF_NJ = D_FF // FF_TN

VMEM_LIMIT = 56 * 1024 * 1024


def _cparams(sem, flags=None):
    return pltpu.CompilerParams(dimension_semantics=sem, vmem_limit_bytes=VMEM_LIMIT, flags=flags)


def _sigmoid(x):
    return 1.0 / (1.0 + jnp.exp(-x))


def _silu(x):
    return x * _sigmoid(x)


def _gelu_tanh(x):
    c = math.sqrt(2.0 / math.pi)
    return 0.5 * x * (1.0 + jnp.tanh(c * (x + 0.044715 * (x * x * x))))


def _log_sigmoid(x):
    return jnp.minimum(x, 0.0) - jnp.log1p(jnp.exp(-jnp.abs(x)))


def _softplus(x):
    return jnp.maximum(x, 0.0) + jnp.log1p(jnp.exp(-jnp.abs(x)))


def _dot(a, b):
    return jnp.dot(a, b, preferred_element_type=F32)


def _dot_nt(a, b):
    return lax.dot_general(a, b, (((1,), (1,)), ((), ())), preferred_element_type=F32)


def _dot_tn(a, b):
    return lax.dot_general(a, b, (((0,), (0,)), ((), ())), preferred_element_type=F32)


def _norm_mod(x, g, sc, sh):
    ms = jnp.mean(x * x, axis=-1, keepdims=True)
    return (x * lax.rsqrt(ms + EPS)) * g * (1.0 + sc) + sh


def _head_norm(o, width):
    ms = jnp.sum(o * o, axis=-1, keepdims=True) * (1.0 / width)
    return o * lax.rsqrt(ms + EPS)


def _mod_kernel(cp_ref, cs_ref, w_ref, b_ref, op_ref, os_ref):
    w = w_ref[...].astype(BF16)
    b = b_ref[...]
    op_ref[...] = _dot(_silu(cp_ref[...]).astype(BF16), w) + b
    os_ref[...] = _dot(_silu(cs_ref[...]).astype(BF16), w) + b


def _mod_call(cp8, cs, w_ada, b_ada):
    tn = 1024
    n = 6 * D
    return pl.pallas_call(
        _mod_kernel,
        grid=(DEPTH, n // tn),
        in_specs=[
            pl.BlockSpec((SUBLANES, D), lambda l, j: (0, 0)),
            pl.BlockSpec((BS, D), lambda l, j: (0, 0)),
            pl.BlockSpec((None, D, tn), lambda l, j: (l, 0, j)),
            pl.BlockSpec((None, 1, tn), lambda l, j: (l, 0, j)),
        ],
        out_specs=[
            pl.BlockSpec((None, SUBLANES, tn), lambda l, j: (l, 0, j)),
            pl.BlockSpec((None, BS, tn), lambda l, j: (l, 0, j)),
        ],
        out_shape=[
            jax.ShapeDtypeStruct((DEPTH, SUBLANES, n), F32),
            jax.ShapeDtypeStruct((DEPTH, BS, n), F32),
        ],
        compiler_params=_cparams(("arbitrary", "arbitrary")),
        name="adaln_mod",
    )(cp8, cs, w_ada, b_ada)


ROW_CHUNK = 128


def _fill_hn(x_ref, g_ref, sc_ref, sh_ref, hn_ref, *, tile, tm, prompt, seq_tiles):
    g = g_ref[...]
    if prompt:
        b = tile // seq_tiles
        sc = sc_ref[pl.ds(b, 1), :]
        sh = sh_ref[pl.ds(b, 1), :]

    def body(r, carry):
        rows = pl.ds(pl.multiple_of(r * ROW_CHUNK, ROW_CHUNK), ROW_CHUNK)
        if prompt:
            xr = x_ref[rows, :]
            ms = jnp.mean(xr * xr, axis=-1, keepdims=True)
            y = (xr * lax.rsqrt(ms + EPS)) * (g * (1.0 + sc)) + sh
        else:
            y = _norm_mod(x_ref[rows, :], g, sc_ref[...], sh_ref[...])
        hn_ref[rows, :] = y.astype(BF16)
        return carry

    lax.fori_loop(0, tm // ROW_CHUNK, body, 0)


def _mod_specs(layer, chunks, prompt, width, col_of):
    rows = SUBLANES if prompt else BS
    per = D // width
    return [pl.BlockSpec((None, rows, width), (lambda i, j, c=c: (layer, 0, c * per + col_of(i, j)))) for c in chunks]


def _proj_kernel(x_ref, g_ref, sc_ref, sh_ref, w_ref, o_ref, hn_ref, *, tm, prompt, seq_tiles):
    @pl.when(pl.program_id(1) == 0)
    def _():
        _fill_hn(x_ref, g_ref, sc_ref, sh_ref, hn_ref, tile=pl.program_id(0), tm=tm, prompt=prompt,
                 seq_tiles=seq_tiles)

    o_ref[...] = _dot_nt(hn_ref[...], w_ref[...])


def _proj_call(x, norm_g, mod, w, *, layer, sc_chunk, sh_chunk, prompt, tm, tn):
    m = x.shape[0]
    n = w.shape[-2]
    kern = functools.partial(_proj_kernel, tm=tm, prompt=prompt, seq_tiles=TP // tm if prompt else 1)
    sc_spec, sh_spec = _mod_specs(layer, (sc_chunk, sh_chunk), prompt, D, lambda i, j: 0)
    return pl.pallas_call(
        kern,
        grid=(m // tm, n // tn),
        in_specs=[
            pl.BlockSpec((tm, D), lambda i, j: (i, 0)),
            pl.BlockSpec((None, 1, D), lambda i, j: (layer, 0, 0)),
            sc_spec,
            sh_spec,
            pl.BlockSpec((None, tn, D), lambda i, j: (layer, j, 0)),
        ],
        out_specs=pl.BlockSpec((tm, tn), lambda i, j: (i, j)),
        out_shape=jax.ShapeDtypeStruct((m, n), F32),
        scratch_shapes=[pltpu.VMEM((tm, D), BF16)],
        compiler_params=_cparams(("arbitrary", "arbitrary")),
        name="in_proj_p" if prompt else "in_proj_s",
    )(x, norm_g, mod, mod, w)


def _resid_kernel(a_ref, w_ref, x_ref, g_ref, o_ref, *, tm, prompt, seq_tiles):
    acc = _dot(a_ref[...].astype(BF16), w_ref[...])
    if prompt:
        b = pl.program_id(0) // seq_tiles
        o_ref[...] = x_ref[...] + g_ref[pl.ds(b, 1), :] * acc
    else:
        g = g_ref[...]
        for t in range(tm // BS):
            rows = slice(t * BS, (t + 1) * BS)
            o_ref[rows, :] = x_ref[rows, :] + g * acc[rows, :]


def _resid_call(a, w, x, mod, *, layer, g_chunk, prompt, tm, tn, name):
    m, k = a.shape
    kern = functools.partial(_resid_kernel, tm=tm, prompt=prompt, seq_tiles=TP // tm if prompt else 1)
    (g_spec,) = _mod_specs(layer, (g_chunk,), prompt, tn, lambda i, j: j)
    return pl.pallas_call(
        kern,
        grid=(m // tm, D // tn),
        in_specs=[
            pl.BlockSpec((tm, k), lambda i, j: (i, 0)),
            pl.BlockSpec((None, k, tn), lambda i, j: (layer, 0, j)),
            pl.BlockSpec((tm, tn), lambda i, j: (i, j)),
            g_spec,
        ],
        out_specs=pl.BlockSpec((tm, tn), lambda i, j: (i, j)),
        out_shape=jax.ShapeDtypeStruct((m, D), F32),
        compiler_params=_cparams(("arbitrary", "arbitrary")),
        name=name,
    )(a, w, x, mod)


def _causal_conv_rows(x_ref, cols, cw, cb, *, n_rows, pre):
    taps = cw.shape[0]
    first_row = lax.broadcasted_iota(jnp.int32, (SUBLANES, 1), 0) == 0
    w = [cw[k:k + 1, :] for k in range(taps)]
    x = x_ref[pre - SUBLANES:pre, cols]
    acc = w[0] * x
    rots = []
    for l in range(1, taps - 1):
        rots.append(pltpu.roll(acc, 1, axis=0))
        acc = w[l] * x + rots[-1]
    rots.append(pltpu.roll(acc, 1, axis=0))
    for g in range(n_rows // SUBLANES):
        x = x_ref[pre + g * SUBLANES:pre + (g + 1) * SUBLANES, cols]
        acc = w[0] * x
        new_rots = []
        for l in range(1, taps):
            new_rots.append(pltpu.roll(acc, 1, axis=0))
            shifted = jnp.where(first_row, rots[l - 1], new_rots[-1])
            acc = (w[l] * x + shifted) if l < taps - 1 else (cb + w[l] * x) + shifted
        yield g, acc
        rots = new_rots


def _ffn_up_kernel(*refs, tm, prompt, seq_tiles, pre, shift):
    (x_ref, g_ref, sc_ref, sh_ref, wg_ref, wv_ref, cwg_ref, cwv_ref, cbg_ref, cbv_ref) = refs[:10]
    if prompt:
        act_ref, stg_ref, stv_ref, hn_ref, ue_ref, carry_ref = refs[10:]
    else:
        s0g_ref, s1g_ref, s0v_ref, s1v_ref, act_ref, stg_ref, stv_ref, hn_ref, ue_ref = refs[10:]
    i = pl.program_id(0)
    j = pl.program_id(1)
    gate_cols = slice(0, FF_TN)
    val_cols = slice(FF_TN, 2 * FF_TN)

    @pl.when(j == 0)
    def _():
        _fill_hn(x_ref, g_ref, sc_ref, sh_ref, hn_ref, tile=i, tm=tm, prompt=prompt, seq_tiles=seq_tiles)

    if prompt:
        first = (i % seq_tiles) == 0

        @pl.when(first)
        def _():
            ue_ref[0:pre, :] = jnp.zeros((pre, 2 * FF_TN), F32)

        @pl.when(jnp.logical_not(first))
        def _():
            ue_ref[0:pre, :] = carry_ref[j]
    else:
        ue_ref[0:BS, gate_cols] = s0g_ref[...]
        ue_ref[0:BS, val_cols] = s0v_ref[...]
        ue_ref[BS:2 * BS, gate_cols] = s1g_ref[...]
        ue_ref[BS:2 * BS, val_cols] = s1v_ref[...]

    hn = hn_ref[...]
    ue_ref[pre:pre + tm, gate_cols] = _dot(hn, wg_ref[...])
    ue_ref[pre:pre + tm, val_cols] = _dot(hn, wv_ref[...])

    cwg, cwv, cbg, cbv = cwg_ref[...], cwv_ref[...], cbg_ref[...], cbv_ref[...]
    if prompt:
        gates = _causal_conv_rows(ue_ref, gate_cols, cwg, cbg, n_rows=tm, pre=pre)
        vals = _causal_conv_rows(ue_ref, val_cols, cwv, cbv, n_rows=tm, pre=pre)
        pending = []
        for (g, gate), (_, val) in zip(gates, vals):
            pending.append(_silu(gate) * val)
            if len(pending) == 2:
                act_ref[(g - 1) * SUBLANES:(g + 1) * SUBLANES, :] = jnp.concatenate(pending, axis=0).astype(BF16)
                pending = []
        carry_ref[j] = ue_ref[tm:tm + pre, :]
        stg_ref[...] = ue_ref[pre + tm - (FFN_K - 1):pre + tm, gate_cols]
        stv_ref[...] = ue_ref[pre + tm - (FFN_K - 1):pre + tm, val_cols]
    else:
        def conv(base, cols, cw, cb):
            u = cb + cw[0:1, :] * ue_ref[base + pre - 2 * shift:base + pre - 2 * shift + rc, cols]
            u = u + cw[1:2, :] * ue_ref[base + pre - shift:base + pre - shift + rc, cols]
            return u + cw[2:3, :] * ue_ref[base + pre:base + pre + rc, cols]

        rc = 16
        for r in range(tm // rc):
            gate = conv(r * rc, gate_cols, cwg, cbg)
            val = conv(r * rc, val_cols, cwv, cbv)
            act_ref[r * rc:(r + 1) * rc, :] = (_silu(gate) * val).astype(BF16)
        for k in range(FFN_K - 1):
            rows = slice(pre + tm - (FFN_K - 1 - k) * BS, pre + tm - (FFN_K - 2 - k) * BS)
            stg_ref[k] = ue_ref[rows, gate_cols]
            stv_ref[k] = ue_ref[rows, val_cols]


def _ffn_up_call(x, norm_g, mod, w, cw, cb, st_in, *, layer, prompt, tm):
    m = x.shape[0]
    seq_tiles = TP // tm if prompt else 1
    pre = SUBLANES if prompt else (FFN_K - 1) * BS
    shift = 1 if prompt else BS
    kern = functools.partial(_ffn_up_kernel, tm=tm, prompt=prompt, seq_tiles=seq_tiles, pre=pre, shift=shift)
    mod_rows = SUBLANES if prompt else BS
    in_specs = [
        pl.BlockSpec((tm, D), lambda i, j: (i, 0)),
        pl.BlockSpec((None, 1, D), lambda i, j: (layer, 0, 0)),
        pl.BlockSpec((None, mod_rows, D), lambda i, j: (layer, 0, 4)),
        pl.BlockSpec((None, mod_rows, D), lambda i, j: (layer, 0, 3)),
        pl.BlockSpec((None, D, FF_TN), lambda i, j: (layer, 0, j)),
        pl.BlockSpec((None, D, FF_TN), lambda i, j: (layer, 0, FF_NJ + j)),
        pl.BlockSpec((None, FFN_K, FF_TN), lambda i, j: (layer, 0, j)),
        pl.BlockSpec((None, FFN_K, FF_TN), lambda i, j: (layer, 0, FF_NJ + j)),
        pl.BlockSpec((None, 1, FF_TN), lambda i, j: (layer, 0, j)),
        pl.BlockSpec((None, 1, FF_TN), lambda i, j: (layer, 0, FF_NJ + j)),
    ]
    args = [x, norm_g, mod, mod, w, w, cw, cw, cb, cb]
    scratch = [pltpu.VMEM((tm, D), BF16), pltpu.VMEM((pre + tm, 2 * FF_TN), F32)]
    if prompt:
        st_spec = pl.BlockSpec((None, FFN_K - 1, FF_TN), lambda i, j: (i, 0, j))
        st_shape = jax.ShapeDtypeStruct((m // tm, FFN_K - 1, D_FF), F32)
        scratch.append(pltpu.VMEM((FF_NJ, pre, 2 * FF_TN), F32))
    else:
        for half in range(2):
            in_specs += [pl.BlockSpec((None, None, BS, FF_TN),
                                      lambda i, j, k=k, half=half: (layer, k, 0, half * FF_NJ + j))
                         for k in range(FFN_K - 1)]
            args += [st_in] * (FFN_K - 1)
        st_spec = pl.BlockSpec((FFN_K - 1, BS, FF_TN), lambda i, j: (0, 0, j))
        st_shape = jax.ShapeDtypeStruct((FFN_K - 1, BS, D_FF), F32)
    return pl.pallas_call(
        kern,
        grid=(m // tm, FF_NJ),
        in_specs=in_specs,
        out_specs=[pl.BlockSpec((tm, FF_TN), lambda i, j: (i, j)), st_spec, st_spec],
        out_shape=[jax.ShapeDtypeStruct((m, D_FF), BF16), st_shape, st_shape],
        scratch_shapes=scratch,
        compiler_params=_cparams(("arbitrary", "arbitrary")),
        name="ffn_up_p" if prompt else "ffn_up_s",
    )(*args)


def _final_kernel(x_ref, g_ref, o_ref):
    x = x_ref[...]
    ms = jnp.mean(x * x, axis=-1, keepdims=True)
    o_ref[...] = (x * lax.rsqrt(ms + EPS)) * g_ref[...]


def _final_call(x, g):
    m = x.shape[0]
    tm = 256
    return pl.pallas_call(
        _final_kernel,
        grid=(m // tm,),
        in_specs=[pl.BlockSpec((tm, D), lambda i: (i, 0)), pl.BlockSpec((1, D), lambda i: (0, 0))],
        out_specs=pl.BlockSpec((tm, D), lambda i: (i, 0)),
        out_shape=jax.ShapeDtypeStruct((m, D), F32),
        compiler_params=_cparams(("arbitrary",)),
        name="final_norm",
    )(x, g)


def _split3(x):
    x1 = x.astype(BF16)
    r = x - x1.astype(F32)
    x2 = r.astype(BF16)
    r = r - x2.astype(F32)
    return x1, x2, r.astype(BF16)


def _ret_log_gamma(h):
    return math.log1p(-(2.0 ** (-5.0 - h)))


def _lru_gates(xc, wa_ref, ba_ref, wx_ref, bx_ref, lam_ref):
    xb = xc.astype(BF16)
    r = _sigmoid(_dot(xb, wa_ref[...]) + ba_ref[...])
    i = _sigmoid(_dot(xb, wx_ref[...]) + bx_ref[...])
    log_a = (-LRU_C * r) * _softplus(-lam_ref[...])
    a = jnp.exp(log_a)
    u = jnp.sqrt(-jnp.tanh(log_a) * (a * a + 1.0)) * (i * xc)
    return a, u


MIX_TC = 256


def _mixer_prompt_kernel(z_ref, cos_ref, sin_ref, wal_ref, bal_ref, gng_ref, lcw_ref, lcb_ref,
                         wa_ref, ba_ref, wx_ref, bx_ref, lam_ref, rng_ref,
                         mix_ref, sg_ref, sr_ref, hl_ref, lc_ref,
                         stg_ref, str_ref, h_ref, lxe_ref, a_ref, u_ref, hs_ref, att_s, rdec_ref):
    t_id = pl.program_id(1)
    nt = pl.num_programs(1)
    tc = MIX_TC
    pre = SUBLANES

    @pl.when(t_id == 0)
    def _():
        stg_ref[...] = jnp.zeros_like(stg_ref)
        str_ref[...] = jnp.zeros_like(str_ref)
        h_ref[...] = jnp.zeros_like(h_ref)
        lxe_ref[0:pre, :] = jnp.zeros((pre, W_LRU), F32)

    lxe_ref[pre:pre + tc, :] = z_ref[:, LX0:LX0 + W_LRU]

    row_c = lax.broadcasted_iota(jnp.int32, (CHUNK, CHUNK), 0)
    col_c = lax.broadcasted_iota(jnp.int32, (CHUNK, CHUNK), 1)
    tri = (col_c <= row_c).astype(BF16)
    rel = (row_c - col_c).astype(F32)
    rowf = lax.broadcasted_iota(jnp.int32, (CHUNK, LANES), 0).astype(F32)
    row1 = lax.broadcasted_iota(jnp.int32, (CHUNK, 1), 0)
    sub_row = lax.broadcasted_iota(jnp.int32, (SUBLANES, 1), 0)
    lane_id = lax.broadcasted_iota(jnp.int32, (SUBLANES, LANES), 1)

    @pl.when(jnp.logical_and(pl.program_id(0) == 0, t_id == 0))
    def _():
        for h in range(H_RET):
            lg = _ret_log_gamma(h)
            rdec_ref[h, 0, :, 0:CHUNK] = jnp.where(rel >= 0, jnp.exp(jnp.maximum(rel, 0.0) * lg), 0.0)
            rdec_ref[h, 1] = jnp.exp((rowf + 1.0) * lg)
            rdec_ref[h, 2] = jnp.exp((CHUNK - 1.0 - rowf) * lg)

    def chunk_body(c, carry):
        r0 = pl.multiple_of(c * CHUNK, CHUNK)
        rows = pl.ds(r0, CHUNK)

        zq = z_ref[rows, Q0:Q0 + H_GLA * DKP] * (DK_GLA ** -0.5)
        zk = z_ref[rows, K0:K0 + H_GLA * DKP]
        lr = z_ref[rows, LR0:LR0 + LANES].astype(BF16)
        la = _log_sigmoid(_dot(lr, wal_ref[...]) + bal_ref[...]) * (1.0 / GLA_TAU)
        p1, p2, p3 = _split3(la)
        bcs = _dot(tri, p1) + _dot(tri, p2) + _dot(tri, p3)
        b_last = bcs[CHUNK - 1:CHUNK, :]
        qb = (zq * jnp.exp(bcs)).astype(BF16)
        kb = (zk * jnp.exp(b_last - bcs)).astype(BF16)
        e_last = jnp.exp(b_last)

        bcs2 = bcs * LOG2E
        for i in range(CHUNK // SUB):
            blk = slice(i * SUB, (i + 1) * SUB)
            q_blk = zq[blk, :]
            b_blk = bcs[blk, :]
            b2_blk = bcs2[blk, :]
            if i > 0:
                r_i = bcs[i * SUB - 1:i * SUB, :]
                q_i = (q_blk * jnp.exp(b_blk - r_i)).astype(BF16)
                kk = (zk * jnp.exp(jnp.where(row1 < i * SUB, r_i - bcs, NEG))).astype(BF16)

            att = [[jnp.zeros((SUBLANES, LANES), F32) for _ in range(SUB // SUBLANES)] for _ in range(H_GLA)]
            for sl in range(SUB):
                s = i * SUB + sl
                for g in range(sl // SUBLANES, SUB // SUBLANES):
                    rows_g = slice(g * SUBLANES, (g + 1) * SUBLANES)
                    diff = b2_blk[rows_g, :] - bcs2[s:s + 1, :]
                    if sl > g * SUBLANES:
                        diff = jnp.where(sub_row >= sl - g * SUBLANES, diff, NEG)
                    w = q_blk[rows_g, :] * zk[s:s + 1, :] * jnp.exp2(diff)
                    for h in range(H_GLA):
                        col = jnp.sum(w[:, h * DKP:(h + 1) * DKP], axis=-1, keepdims=True)
                        att[h][g] = jnp.where(lane_id == s, col, att[h][g])
            for h in range(H_GLA):
                a_h = jnp.concatenate(att[h], axis=0)[:, :CHUNK]
                if i > 0:
                    a_h = a_h + _dot_nt(q_i[:, h * DKP:(h + 1) * DKP], kk[:, h * DKP:(h + 1) * DKP])
                att_s[h, blk, 0:CHUNK] = a_h

        for h in range(H_GLA):
            v_h = z_ref[rows, V0 + h * DVP:V0 + (h + 1) * DVP].astype(BF16)
            st_old = stg_ref[h]
            o_h = _dot(att_s[h, :, 0:CHUNK].astype(BF16), v_h)
            o_h = o_h + _dot_nt(qb[:, h * DKP:(h + 1) * DKP], st_old.astype(BF16))
            stg_ref[h] = e_last[:, h * DKP:(h + 1) * DKP] * st_old + _dot_tn(v_h, kb[:, h * DKP:(h + 1) * DKP])
            gate = z_ref[rows, GG0 + h * DVP:GG0 + (h + 1) * DVP]
            y = _head_norm(o_h, DV_GLA) * gng_ref[:, h * DVP:(h + 1) * DVP] * _silu(gate)
            mix_ref[rows, MG0 + h * DVP:MG0 + (h + 1) * DVP] = y.astype(BF16)

        cosv = cos_ref[rows, :]
        sinv = sin_ref[rows, :]
        for h in range(H_RET):
            lg = _ret_log_gamma(h)
            cols = slice(h * DK_RET, (h + 1) * DK_RET)
            xq = z_ref[rows, RQ0 + h * DK_RET:RQ0 + (h + 1) * DK_RET]
            xk = z_ref[rows, RK0 + h * DK_RET:RK0 + (h + 1) * DK_RET]
            q = (xq * cosv + pltpu.roll(xq, DK_RET // 2, axis=1) * sinv) * (DK_RET ** -0.5)
            k = xk * cosv + pltpu.roll(xk, DK_RET // 2, axis=1) * sinv
            v = z_ref[rows, RV0 + h * DK_RET:RV0 + (h + 1) * DK_RET].astype(BF16)
            decay = rdec_ref[h, 0, :, 0:CHUNK]
            q_dec = rdec_ref[h, 1]
            k_dec = rdec_ref[h, 2]
            c_dec = math.exp(CHUNK * lg)
            qb16 = q.astype(BF16)
            att_r = _dot_nt(qb16, k.astype(BF16)) * decay
            s_old = str_ref[h]
            o = _dot(att_r.astype(BF16), v) + _dot(qb16, s_old.astype(BF16)) * q_dec
            str_ref[h] = c_dec * s_old + _dot_tn((k * k_dec).astype(BF16), v)
            gate = z_ref[rows, RG0 + h * DK_RET:RG0 + (h + 1) * DK_RET]
            y = _head_norm(o, DK_RET) * rng_ref[:, cols] * _silu(gate)
            mix_ref[rows, MR0 + h * DK_RET:MR0 + (h + 1) * DK_RET] = y.astype(BF16)

        return carry

    lax.fori_loop(0, tc // CHUNK, chunk_body, 0, unroll=4)

    for g, y in _causal_conv_rows(lxe_ref, slice(0, W_LRU), lcw_ref[...], lcb_ref[...], n_rows=tc, pre=pre):
        hs_ref[g * SUBLANES:(g + 1) * SUBLANES, :] = y
    a, u = _lru_gates(hs_ref[...], wa_ref, ba_ref, wx_ref, bx_ref, lam_ref)
    a_ref[...] = a
    u_ref[...] = u

    def scan_body(t, h):
        h = a_ref[pl.ds(t, 1), :] * h + u_ref[pl.ds(t, 1), :]
        hs_ref[pl.ds(t, 1), :] = h
        return h

    del scan_body
    row8 = lax.broadcasted_iota(jnp.int32, (SUBLANES, 1), 0)

    def blk_body(j, h):
        rows8 = pl.ds(pl.multiple_of(j * SUBLANES, SUBLANES), SUBLANES)
        aa = a_ref[rows8, :]
        uu = u_ref[rows8, :]
        for d in (1, 2, 4):
            keep = row8 >= d
            a_sh = jnp.where(keep, pltpu.roll(aa, d, axis=0), 1.0)
            u_sh = jnp.where(keep, pltpu.roll(uu, d, axis=0), 0.0)
            uu = aa * u_sh + uu
            aa = aa * a_sh
        hb = aa * h + uu
        hs_ref[rows8, :] = hb
        return hb[SUBLANES - 1:SUBLANES, :]

    h_fin = lax.fori_loop(0, tc // SUBLANES, blk_body, h_ref[0:1, :], unroll=4)
    h_ref[0:1, :] = h_fin

    def out_body(c, carry):
        rows = pl.ds(pl.multiple_of(c * CHUNK, CHUNK), CHUNK)
        y = hs_ref[rows, :] * _gelu_tanh(z_ref[rows, LG0:LG0 + W_LRU])
        mix_ref[rows, ML0:ML0 + W_LRU] = y.astype(BF16)
        return carry

    lax.fori_loop(0, tc // CHUNK, out_body, 0)

    lxe_ref[0:pre, :] = lxe_ref[tc:tc + pre, :]

    @pl.when(t_id == nt - 1)
    def _():
        for h in range(H_GLA):
            sg_ref[h] = stg_ref[h].T[:DK_GLA, :DV_GLA]
        sr_ref[...] = str_ref[...]
        hl_ref[...] = h_fin
        lc_ref[...] = lxe_ref[tc + pre - (LRU_K - 1):tc + pre, :]


def _mixer_prompt_call(z, cos_t, sin_t, pw, *, layer):
    tc = MIX_TC
    nt = TP // tc

    def lspec(shape):
        return pl.BlockSpec((None,) + shape, lambda b, t: (layer,) + (0,) * len(shape))

    in_specs = [
        pl.BlockSpec((tc, NZ), lambda b, t: (b * nt + t, 0)),
        pl.BlockSpec((tc, LANES), lambda b, t: (t, 0)),
        pl.BlockSpec((tc, LANES), lambda b, t: (t, 0)),
        lspec((LANES, H_GLA * DKP)), lspec((1, H_GLA * DKP)), lspec((1, H_GLA * DVP)),
        lspec((LRU_K, W_LRU)), lspec((1, W_LRU)),
        lspec((W_LRU, W_LRU)), lspec((1, W_LRU)), lspec((W_LRU, W_LRU)), lspec((1, W_LRU)), lspec((1, W_LRU)),
        lspec((1, H_RET * DK_RET)),
    ]
    out_specs = [
        pl.BlockSpec((tc, NMIX), lambda b, t: (b * nt + t, 0)),
        pl.BlockSpec((None, H_GLA, DK_GLA, DV_GLA), lambda b, t: (b, 0, 0, 0)),
        pl.BlockSpec((None, H_RET, DK_RET, DK_RET), lambda b, t: (b, 0, 0, 0)),
        pl.BlockSpec((None, 1, W_LRU), lambda b, t: (b, 0, 0)),
        pl.BlockSpec((None, LRU_K - 1, W_LRU), lambda b, t: (b, 0, 0)),
    ]
    out_shape = [
        jax.ShapeDtypeStruct((BP * TP, NMIX), BF16),
        jax.ShapeDtypeStruct((BP, H_GLA, DK_GLA, DV_GLA), F32),
        jax.ShapeDtypeStruct((BP, H_RET, DK_RET, DK_RET), F32),
        jax.ShapeDtypeStruct((BP, 1, W_LRU), F32),
        jax.ShapeDtypeStruct((BP, LRU_K - 1, W_LRU), F32),
    ]
    scratch = [
        pltpu.VMEM((H_GLA, DVP, DKP), F32),
        pltpu.VMEM((H_RET, DK_RET, DK_RET), F32),
        pltpu.VMEM((SUBLANES, W_LRU), F32),
        pltpu.VMEM((SUBLANES + tc, W_LRU), F32),
        pltpu.VMEM((tc, W_LRU), F32), pltpu.VMEM((tc, W_LRU), F32), pltpu.VMEM((tc, W_LRU), F32),
        pltpu.VMEM((H_GLA, CHUNK, LANES), F32),
        pltpu.VMEM((H_RET, 3, CHUNK, LANES), F32),
    ]
    return pl.pallas_call(
        _mixer_prompt_kernel,
        grid=(BP, nt),
        in_specs=in_specs,
        out_specs=out_specs,
        out_shape=out_shape,
        scratch_shapes=scratch,
        compiler_params=_cparams(("arbitrary", "arbitrary")),
        name="mixer_p",
    )(z, cos_t, sin_t, pw["w_alpha"], pw["b_alpha"], pw["gla_norm_g"], pw["lru_conv_w"], pw["lru_conv_b"],
      pw["lru_wa"], pw["lru_ba"], pw["lru_wx"], pw["lru_bx"], pw["lru_lam"], pw["ret_norm_g"])


MIX_BB = 8


def _mixer_sample_kernel(z_ref, cos_ref, sin_ref, wal_ref, bal_ref, gng_ref, lcw_ref, lcb_ref,
                         wa_ref, ba_ref, wx_ref, bx_ref, lam_ref, rng_ref,
                         sg0_ref, sr0_ref, h0_ref, lc0_ref,
                         mix_ref, sg_ref, sr_ref, hl_ref, lc_ref,
                         qb_s, kb_s, v_s, og_s, rq_s, rk_s, rv_s, or_s, sp_ref, et_ref):
    bb = MIX_BB
    nrow = TS * bb

    @pl.when(pl.program_id(0) == 0)
    def _():
        sp_ref[...] = jnp.zeros_like(sp_ref)
        et_ref[...] = jnp.zeros_like(et_ref)

    def zcols(c0, width):
        return z_ref[:, :, c0:c0 + width].reshape(nrow, width)

    def slab(x, t):
        return x[t * bb:(t + 1) * bb, :]

    zq = zcols(Q0, H_GLA * DKP) * (DK_GLA ** -0.5)
    zk = zcols(K0, H_GLA * DKP)
    zv = zcols(V0, H_GLA * DVP)
    la = _log_sigmoid(_dot(zcols(LR0, LANES).astype(BF16), wal_ref[...]) + bal_ref[...]) * (1.0 / GLA_TAU)
    bs = [slab(la, 0)]
    for t in range(1, TS):
        bs.append(bs[-1] + slab(la, t))
    b_last = bs[-1]
    e_last = jnp.exp(b_last)
    o_t = []
    for t in range(TS):
        acc = None
        for s in range(t + 1):
            w = slab(zq, t) * slab(zk, s) * jnp.exp(bs[t] - bs[s])
            parts = []
            for h in range(H_GLA):
                a_ts = jnp.sum(w[:, h * DKP:(h + 1) * DKP], axis=-1, keepdims=True)
                parts.append(a_ts * slab(zv, s)[:, h * DVP:(h + 1) * DVP])
            contrib = jnp.concatenate(parts, axis=1)
            acc = contrib if acc is None else acc + contrib
        o_t.append(acc)
    o_intra = jnp.concatenate(o_t, axis=0)
    b_all = jnp.concatenate(bs, axis=0)
    qb = zq * jnp.exp(b_all)
    kb = zk * jnp.exp(jnp.concatenate([b_last] * TS, axis=0) - b_all)
    for h in range(H_GLA):
        qb_s[h] = qb[:, h * DKP:(h + 1) * DKP]
        kb_s[h] = kb[:, h * DKP:(h + 1) * DKP]
        for p in range(DVP // LANES):
            v_s[h * (DVP // LANES) + p] = zv[:, h * DVP + p * LANES:h * DVP + (p + 1) * LANES]
        et_ref[h, 0:bb, :] = e_last[:, h * DKP:(h + 1) * DKP]
    e_cols = [et_ref[h].T for h in range(H_GLA)]

    for b in range(bb):
        seq = pl.ds(b, TS, stride=bb)
        for h in range(H_GLA):
            s0 = sg0_ref[b, h]
            sp_ref[0:DK_GLA, 0:DV_GLA] = s0
            q_bh = qb_s[h, seq, :]
            k_bh = kb_s[h, seq, :]
            v_bh = jnp.concatenate([v_s[h * (DVP // LANES) + p, seq, :] for p in range(DVP // LANES)], axis=1)
            o_bh = _dot(q_bh, sp_ref[...])
            for p in range(DVP // LANES):
                og_s[h * (DVP // LANES) + p, seq, :] = o_bh[:, p * LANES:(p + 1) * LANES]
            upd = _dot_tn(k_bh, v_bh)
            s_new = e_cols[h][:, b:b + 1] * sp_ref[...] + upd
            sg_ref[b, h] = s_new[:DK_GLA, :DV_GLA]

    og = jnp.concatenate([og_s[i] for i in range(H_GLA * DVP // LANES)], axis=1) + o_intra
    for h in range(H_GLA):
        cols = slice(h * DVP, (h + 1) * DVP)
        y = _head_norm(og[:, cols], DV_GLA) * gng_ref[:, cols] * _silu(zcols(GG0 + h * DVP, DVP))
        mix_ref[:, :, MG0 + h * DVP:MG0 + (h + 1) * DVP] = y.reshape(TS, bb, DVP)

    cosv = jnp.concatenate([jnp.broadcast_to(cos_ref[t:t + 1, :], (bb, LANES)) for t in range(TS)], axis=0)
    sinv = jnp.concatenate([jnp.broadcast_to(sin_ref[t:t + 1, :], (bb, LANES)) for t in range(TS)], axis=0)
    trow = lax.broadcasted_iota(jnp.int32, (TS, LANES), 0).astype(F32)
    r_intra = []
    for h in range(H_RET):
        lg = _ret_log_gamma(h)
        xq = zcols(RQ0 + h * DK_RET, DK_RET)
        xk = zcols(RK0 + h * DK_RET, DK_RET)
        q = (xq * cosv + pltpu.roll(xq, DK_RET // 2, axis=1) * sinv) * (DK_RET ** -0.5)
        k = xk * cosv + pltpu.roll(xk, DK_RET // 2, axis=1) * sinv
        v = zcols(RV0 + h * DK_RET, DK_RET)
        rq_s[h] = q
        rk_s[h] = k
        rv_s[h] = v
        outs = []
        for t in range(TS):
            acc = None
            for s in range(t + 1):
                a_ts = jnp.sum(slab(q, t) * slab(k, s), axis=-1, keepdims=True) * math.exp((t - s) * lg)
                contrib = a_ts * slab(v, s)
                acc = contrib if acc is None else acc + contrib
            outs.append(acc)
        r_intra.append(jnp.concatenate(outs, axis=0))

    for b in range(bb):
        seq = pl.ds(b, TS, stride=bb)
        for h in range(H_RET):
            lg = _ret_log_gamma(h)
            q_dec = jnp.exp((trow + 1.0) * lg)
            k_dec = jnp.exp((TS - 1.0 - trow) * lg)
            s0 = sr0_ref[b, h]
            q_bh = rq_s[h, seq, :]
            k_bh = rk_s[h, seq, :]
            v_bh = rv_s[h, seq, :]
            or_s[h, seq, :] = _dot(q_bh, s0) * q_dec
            sr_ref[b, h] = math.exp(TS * lg) * s0 + _dot_tn(k_bh * k_dec, v_bh)

    for h in range(H_RET):
        cols = slice(h * DK_RET, (h + 1) * DK_RET)
        o = or_s[h] + r_intra[h]
        y = _head_norm(o, DK_RET) * rng_ref[:, cols] * _silu(zcols(RG0 + h * DK_RET, DK_RET))
        mix_ref[:, :, MR0 + h * DK_RET:MR0 + (h + 1) * DK_RET] = y.reshape(TS, bb, DK_RET)

    lx = zcols(LX0, W_LRU)
    xe = [lc0_ref[:, k * W_LRU:(k + 1) * W_LRU] for k in range(LRU_K - 1)] + [slab(lx, t) for t in range(TS)]
    cw = lcw_ref[...]
    xc = []
    for t in range(TS):
        acc = lcb_ref[...] + cw[0:1, :] * xe[t]
        for k in range(1, LRU_K):
            acc = acc + cw[k:k + 1, :] * xe[t + k]
        xc.append(acc)
    a, u = _lru_gates(jnp.concatenate(xc, axis=0), wa_ref, ba_ref, wx_ref, bx_ref, lam_ref)
    h = h0_ref[...]
    hs = []
    for t in range(TS):
        h = slab(a, t) * h + slab(u, t)
        hs.append(h)
    y = jnp.concatenate(hs, axis=0) * _gelu_tanh(zcols(LG0, W_LRU))
    mix_ref[:, :, ML0:ML0 + W_LRU] = y.reshape(TS, bb, W_LRU)
    hl_ref[...] = h
    for k in range(LRU_K - 1):
        lc_ref[:, k * W_LRU:(k + 1) * W_LRU] = xe[TS + k]


def _mixer_sample_call(z, cos_t, sin_t, pw, st_gla, st_ret, st_lru, st_lconv, *, layer):
    bb = MIX_BB

    def lspec(shape):
        return pl.BlockSpec((None,) + shape, lambda j: (layer,) + (0,) * len(shape))

    in_specs = [
        pl.BlockSpec((TS, bb, NZ), lambda j: (0, j, 0)),
        pl.BlockSpec((TS, LANES), lambda j: (0, 0)),
        pl.BlockSpec((TS, LANES), lambda j: (0, 0)),
        lspec((LANES, H_GLA * DKP)), lspec((1, H_GLA * DKP)), lspec((1, H_GLA * DVP)),
        lspec((LRU_K, W_LRU)), lspec((1, W_LRU)),
        lspec((W_LRU, W_LRU)), lspec((1, W_LRU)), lspec((W_LRU, W_LRU)), lspec((1, W_LRU)), lspec((1, W_LRU)),
        lspec((1, H_RET * DK_RET)),
        pl.BlockSpec((None, bb, H_GLA, DK_GLA, DV_GLA), lambda j: (layer, j, 0, 0, 0)),
        pl.BlockSpec((None, bb, H_RET, DK_RET, DK_RET), lambda j: (layer, j, 0, 0, 0)),
        pl.BlockSpec((None, bb, W_LRU), lambda j: (layer, j, 0)),
        pl.BlockSpec((None, bb, (LRU_K - 1) * W_LRU), lambda j: (layer, j, 0)),
    ]
    out_specs = [
        pl.BlockSpec((TS, bb, NMIX), lambda j: (0, j, 0)),
        pl.BlockSpec((bb, H_GLA, DK_GLA, DV_GLA), lambda j: (j, 0, 0, 0)),
        pl.BlockSpec((bb, H_RET, DK_RET, DK_RET), lambda j: (j, 0, 0, 0)),
        pl.BlockSpec((bb, W_LRU), lambda j: (j, 0)),
        pl.BlockSpec((bb, (LRU_K - 1) * W_LRU), lambda j: (j, 0)),
    ]
    out_shape = [
        jax.ShapeDtypeStruct((TS, BS, NMIX), F32),
        jax.ShapeDtypeStruct((BS, H_GLA, DK_GLA, DV_GLA), F32),
        jax.ShapeDtypeStruct((BS, H_RET, DK_RET, DK_RET), F32),
        jax.ShapeDtypeStruct((BS, W_LRU), F32),
        jax.ShapeDtypeStruct((BS, (LRU_K - 1) * W_LRU), F32),
    ]
    nrow = TS * bb
    scratch = [
        pltpu.VMEM((H_GLA, nrow, LANES), F32), pltpu.VMEM((H_GLA, nrow, LANES), F32),
        pltpu.VMEM((H_GLA * DVP // LANES, nrow, LANES), F32), pltpu.VMEM((H_GLA * DVP // LANES, nrow, LANES), F32),
        pltpu.VMEM((H_RET, nrow, LANES), F32), pltpu.VMEM((H_RET, nrow, LANES), F32),
        pltpu.VMEM((H_RET, nrow, LANES), F32), pltpu.VMEM((H_RET, nrow, LANES), F32),
        pltpu.VMEM((DKP, DVP), F32),
        pltpu.VMEM((H_GLA, LANES, LANES), F32),
    ]
    return pl.pallas_call(
        _mixer_sample_kernel,
        grid=(BS // bb,),
        in_specs=in_specs,
        out_specs=out_specs,
        out_shape=out_shape,
        scratch_shapes=scratch,
        compiler_params=_cparams(("arbitrary",)),
        name="mixer_s",
    )(z, cos_t, sin_t, pw["w_alpha"], pw["b_alpha"], pw["gla_norm_g"], pw["lru_conv_w"], pw["lru_conv_b"],
      pw["lru_wa"], pw["lru_ba"], pw["lru_wx"], pw["lru_bx"], pw["lru_lam"], pw["ret_norm_g"],
      st_gla, st_ret, st_lru, st_lconv)


def _pad_heads(w, heads, d, dp):
    lead = w.shape[:-1]
    w = w.reshape(lead + (heads, d))
    w = jnp.pad(w, [(0, 0)] * len(lead) + [(0, 0), (0, dp - d)])
    return w.reshape(lead + (heads * dp,))


def _split_cols(w, sizes):
    out, off = [], 0
    for s in sizes:
        out.append(w[..., off:off + s])
        off += s
    return out


def _rope_tables(start, length):
    half = DK_RET // 2
    freqs = ROPE_BASE ** (-jnp.arange(half, dtype=F32) / half)
    pos = start + jnp.arange(length, dtype=jnp.int32)
    ang = pos.astype(F32)[:, None] * freqs[None, :]
    cos, sin = jnp.cos(ang), jnp.sin(ang)
    return jnp.concatenate([cos, cos], axis=1), jnp.concatenate([-sin, sin], axis=1)


def _prep_weights(w_in, gla_w_alpha, gla_b_alpha, gla_norm_g, lru_conv_w, lru_conv_b, lru_w_a, lru_b_a,
                  lru_w_x, lru_b_x, lru_lambda, ret_norm_g, w_out, ffn_w_up, ffn_conv_w, ffn_conv_b, ffn_w_down):
    w_gla, w_ret = H_GLA * DV_GLA, H_RET * DK_RET
    w_rows = jnp.swapaxes(w_in, 1, 2).astype(BF16)
    pieces, off = [], 0

    def take(n_rows):
        nonlocal off
        blk = w_rows[:, off:off + n_rows]
        off += n_rows
        return blk

    def zeros(n_rows):
        return jnp.zeros((DEPTH, n_rows, D), BF16)

    for heads, d, dp in ((H_GLA, DK_GLA, DKP), (H_GLA, DK_GLA, DKP), (H_GLA, DV_GLA, DVP), (1, GLA_RANK, LANES),
                         (H_GLA, DV_GLA, DVP)):
        for _ in range(heads):
            pieces += [take(d), zeros(dp - d)]
    pieces += [take(2 * W_LRU + 4 * w_ret), zeros(NZ - NZ_USED)]
    w_in_p = jnp.concatenate(pieces, axis=1)
    w_alpha = jnp.pad(_pad_heads(gla_w_alpha, H_GLA, DK_GLA, DKP), ((0, 0), (0, LANES - GLA_RANK), (0, 0))).astype(BF16)
    eye = jnp.eye(H_LRU, dtype=F32)

    def block_diag(w):
        return jnp.einsum("lhij,hg->lhigj", w, eye).reshape(DEPTH, W_LRU, W_LRU).astype(BF16)

    wo_g, wo_l, wo_r = _split_cols(jnp.swapaxes(w_out, 1, 2), (w_gla, W_LRU, w_ret))
    w_out_p = jnp.swapaxes(jnp.concatenate([_pad_heads(wo_g, H_GLA, DV_GLA, DVP), wo_l, wo_r], axis=-1), 1, 2)
    return dict(
        w_in=w_in_p,
        w_alpha=w_alpha,
        b_alpha=_pad_heads(gla_b_alpha, H_GLA, DK_GLA, DKP)[:, None, :],
        gla_norm_g=_pad_heads(gla_norm_g, H_GLA, DV_GLA, DVP)[:, None, :],
        lru_conv_w=lru_conv_w,
        lru_conv_b=lru_conv_b[:, None, :],
        lru_wa=block_diag(lru_w_a), lru_ba=lru_b_a[:, None, :],
        lru_wx=block_diag(lru_w_x), lru_bx=lru_b_x[:, None, :],
        lru_lam=lru_lambda[:, None, :],
        ret_norm_g=ret_norm_g[:, None, :],
        w_out=w_out_p.astype(BF16),
        w_up=ffn_w_up.astype(BF16),
        ffn_cw=ffn_conv_w,
        ffn_cb=ffn_conv_b[:, None, :],
        w_down=ffn_w_down.astype(BF16),
    )


def _run_group(x, mod, pw, norm1_g, norm2_g, tables, states, *, prompt):
    tm = 1024
    tn_in, tn_out, tn_down = 1792, 1024, 512
    outs = []
    for l in range(DEPTH):
        z = _proj_call(x, norm1_g, mod, pw["w_in"], layer=l, sc_chunk=1, sh_chunk=0, prompt=prompt, tm=tm, tn=tn_in)
        if prompt:
            mixed, s_gla, s_ret, s_lru, s_lconv = _mixer_prompt_call(z, *tables, pw, layer=l)
            s_lru = s_lru.reshape(BP, W_LRU)
        else:
            st_gla, st_ret, st_lru, st_lconv, _ = states
            mixed, s_gla, s_ret, s_lru, s_lconv = _mixer_sample_call(
                z.reshape(TS, BS, NZ), *tables, pw, st_gla, st_ret, st_lru, st_lconv, layer=l)
            mixed = mixed.reshape(TS * BS, NMIX)
            s_lconv = s_lconv.reshape(BS, LRU_K - 1, W_LRU)
        x = _resid_call(mixed, pw["w_out"], x, mod, layer=l, g_chunk=2, prompt=prompt, tm=tm, tn=tn_out,
                        name="out_proj_p" if prompt else "out_proj_s")
        act, s_fg, s_fv = _ffn_up_call(x, norm2_g, mod, pw["w_up"], pw["ffn_cw"], pw["ffn_cb"],
                                       None if prompt else states[4], layer=l, prompt=prompt, tm=tm)
        x = _resid_call(act, pw["w_down"], x, mod, layer=l, g_chunk=5, prompt=prompt, tm=tm, tn=tn_down,
                        name="ffn_down_p" if prompt else "ffn_down_s")
        s_fconv = jnp.concatenate([s_fg, s_fv], axis=-1)
        if prompt:
            s_fconv = s_fconv[TP // tm - 1::TP // tm]
        else:
            s_fconv = jnp.swapaxes(s_fconv, 0, 1)
        outs.append((s_gla, s_ret, s_lru, s_lconv, s_fconv))
    return x, [jnp.stack(o) for o in zip(*outs)]


def kernel(x_prompt, x_sample, state_gla, state_ret, state_lru, state_lru_conv, state_ffn_conv, c_prompt, c_sample,
           norm1_g, norm2_g, final_g, w_ada, b_ada, w_in, gla_w_alpha, gla_b_alpha, gla_norm_g, lru_conv_w,
           lru_conv_b, lru_w_a, lru_b_a, lru_w_x, lru_b_x, lru_lambda, ret_norm_g, w_out, ffn_w_up, ffn_conv_w,
           ffn_conv_b, ffn_w_down):
    pw = _prep_weights(w_in, gla_w_alpha, gla_b_alpha, gla_norm_g, lru_conv_w, lru_conv_b, lru_w_a, lru_b_a,
                       lru_w_x, lru_b_x, lru_lambda, ret_norm_g, w_out, ffn_w_up, ffn_conv_w, ffn_conv_b, ffn_w_down)
    cp8 = jnp.pad(c_prompt, ((0, SUBLANES - BP), (0, 0)))
    mod_p, mod_s = _mod_call(cp8, c_sample, w_ada, b_ada[:, None, :])
    n1 = norm1_g[:, None, :]
    n2 = norm2_g[:, None, :]

    xp = x_prompt.reshape(BP * TP, D)
    xs = jnp.swapaxes(x_sample, 0, 1).reshape(TS * BS, D)
    st_s = (state_gla, state_ret, state_lru, state_lru_conv.reshape(DEPTH, BS, (LRU_K - 1) * W_LRU),
            jnp.swapaxes(state_ffn_conv, 1, 2))

    xp, outs_p = _run_group(xp, mod_p, pw, n1, n2, _rope_tables(0, TP), None, prompt=True)
    xs, outs_s = _run_group(xs, mod_s, pw, n1, n2, _rope_tables(PAST, TS), st_s, prompt=False)

    fg = final_g[None, :]
    y_p = _final_call(xp, fg).reshape(BP, TP, D)
    y_s = jnp.swapaxes(_final_call(xs, fg).reshape(TS, BS, D), 0, 1)
    return (y_p, y_s, *outs_p, *outs_s)
```
